```python
import jax, jax.numpy as jnp
from jax import lax
import numpy as np

D_MODEL = 1024
BATCH = 2
SEQ = 8192
DEPTH = 4
DEC_BATCH = 128
DEC_SEQ = 1
PAST_LEN = 2048
PAGE_SIZE = 128

N_A = DEPTH // 2
N_B = DEPTH - N_A
H_A = 4
DK_A = D_MODEL // 2 // H_A
DV_A = D_MODEL // H_A
DK_TOT = H_A * DK_A
DV_TOT = H_A * DV_A
GATE_RANK = 16
GATE_TAU = 16.0
GLA_CHUNK = 64
GLA_IN = 2 * DK_TOT + 2 * DV_TOT + GATE_RANK
HD_B = 64
H_B = D_MODEL // HD_B
D_B = H_B * HD_B
KV_OUT = 2 * D_B + H_B
Q_BLOCK = 128
EPS = 1e-6

kernel_name = "yoco_gla_fox_decoder_step"


def rms_norm(x):
    xf = x.astype(jnp.float32)
    return (xf * lax.rsqrt(jnp.mean(xf * xf, axis=-1, keepdims=True) + EPS)).astype(x.dtype)


def ada_params(c, w, b, n):
    mod = jax.nn.silu(c) @ w + b
    return jnp.split(mod[:, None, :], n, axis=-1)


def gla_scan(q, k, v, loga, s0, chunk):
    f32 = jnp.float32
    B, L, H, DK = q.shape
    DV = v.shape[-1]
    n = L // chunk

    def to_chunks(t):
        return t.astype(f32).reshape(B, n, chunk, H, t.shape[-1]).transpose(1, 0, 3, 2, 4)

    qc, kc, vc, gc = to_chunks(q), to_chunks(k), to_chunks(v), to_chunks(loga)
    causal = jnp.tril(jnp.ones((chunk, chunk), dtype=bool))[None, None, :, :, None]

    def step(s, inp):
        qi, ki, vi, gi = inp
        b = jnp.cumsum(gi, axis=2)
        diff = jnp.where(causal, b[:, :, :, None, :] - b[:, :, None, :, :], -jnp.inf)
        att = jnp.einsum('bhik,bhjk,bhijk->bhij', qi, ki, jnp.exp(diff))
        o = jnp.einsum('bhij,bhjv->bhiv', att, vi) + jnp.einsum('bhik,bhkv->bhiv', qi * jnp.exp(b), s)
        b_last = b[:, :, -1:, :]
        s_new = jnp.exp(b_last[:, :, 0, :, None]) * s + jnp.einsum('bhjk,bhjv->bhkv', ki * jnp.exp(b_last - b), vi)
        return s_new, o

    s_fin, oc = lax.scan(step, s0.astype(f32), (qc, kc, vc, gc))
    o = oc.transpose(1, 0, 3, 2, 4).reshape(B, L, H, DV)
    return o, s_fin


def gla_mixer(h, s0, w_in, w_g2, b_g, onorm, w_out, chunk):
    B, L, _ = h.shape
    proj = h @ w_in
    q, k, v, g, z = jnp.split(proj, [DK_TOT, 2 * DK_TOT, 2 * DK_TOT + DV_TOT, 2 * DK_TOT + 2 * DV_TOT], axis=-1)
    q = q.reshape(B, L, H_A, DK_A) * (DK_A ** -0.5)
    k = k.reshape(B, L, H_A, DK_A)
    v = v.reshape(B, L, H_A, DV_A)
    loga = (jax.nn.log_sigmoid((z @ w_g2 + b_g).astype(jnp.float32)) / GATE_TAU).reshape(B, L, H_A, DK_A)
    o, s = gla_scan(q, k, v, loga, s0, chunk)
    o = rms_norm(o.astype(h.dtype)) * onorm
    o = o.reshape(B, L, DV_TOT) * jax.nn.silu(g)
    return o @ w_out, s


def fox_shared_kv(x, c, kv_ada_w, kv_ada_b, w_kv, b_f, k_norm):
    B, L, _ = x.shape
    shift, scale = ada_params(c, kv_ada_w, kv_ada_b, 2)
    u = rms_norm(x) * (1 + scale) + shift
    p = u @ w_kv
    k = rms_norm(p[..., :D_B].reshape(B, L, H_B, HD_B)) * k_norm
    v = p[..., D_B:2 * D_B].reshape(B, L, H_B, HD_B)
    logf = jax.nn.log_sigmoid((p[..., 2 * D_B:] + b_f).astype(jnp.float32))
    return k, v, logf


def fox_mixer(h, k_all, v_all, f_all, past, w_in, q_norm, w_out):
    B, L, _ = h.shape
    T = k_all.shape[1]
    proj = h @ w_in
    q = rms_norm(proj[..., :D_B].reshape(B, L, H_B, HD_B)) * q_norm * (HD_B ** -0.5)
    g = proj[..., D_B:]
    f_q = f_all[:, past:]
    f_k = f_all.transpose(0, 2, 1)
    qb = Q_BLOCK if L % Q_BLOCK == 0 else L
    nb = L // qb
    kpos = jnp.arange(T)

    def block(args):
        i, q_blk, fq_blk = args
        qpos = past + i * qb + jnp.arange(qb)
        s = jnp.einsum('bqhd,bkhd->bhqk', q_blk, k_all, preferred_element_type=jnp.float32)
        s = s + fq_blk.transpose(0, 2, 1)[..., None] - f_k[:, :, None, :]
        s = jnp.where((kpos[None, :] <= qpos[:, None])[None, None], s, -jnp.inf)
        p = jax.nn.softmax(s, axis=-1)
        return jnp.einsum('bhqk,bkhd->bqhd', p.astype(v_all.dtype), v_all)

    q_blocks = q.reshape(B, nb, qb, H_B, HD_B).swapaxes(0, 1)
    fq_blocks = f_q.reshape(B, nb, qb, H_B).swapaxes(0, 1)
    o = lax.map(block, (jnp.arange(nb), q_blocks, fq_blocks))
    o = o.swapaxes(0, 1).reshape(B, L, D_B)
    return (o * jax.nn.silu(g)) @ w_out


def trunk(x, c, s_gla, k_past, v_past, logf_past, gla_chunk, ada_w, ada_b, gla_w_in, gla_w_g2, gla_b_g,
          gla_onorm, gla_w_out, kv_ada_w, kv_ada_b, w_kv, b_f, k_norm, fox_w_in, q_norm, fox_w_out):
    past = k_past.shape[1]
    new_s = []
    for layer in range(DEPTH):
        if layer == N_A:
            k_new, v_new, logf_new = fox_shared_kv(x, c, kv_ada_w, kv_ada_b, w_kv, b_f, k_norm)
            k_all = jnp.concatenate([k_past.astype(k_new.dtype), k_new], axis=1)
            v_all = jnp.concatenate([v_past.astype(v_new.dtype), v_new], axis=1)
            f_all = jnp.cumsum(jnp.concatenate([logf_past.astype(jnp.float32), logf_new], axis=1), axis=1)
        shift, scale, gate = ada_params(c, ada_w[layer], ada_b[layer], 3)
        h = rms_norm(x) * (1 + scale) + shift
        if layer < N_A:
            out, s = gla_mixer(h, s_gla[layer], gla_w_in[layer], gla_w_g2[layer], gla_b_g[layer],
                               gla_onorm[layer], gla_w_out[layer], gla_chunk)
            new_s.append(s)
        else:
            j = layer - N_A
            out = fox_mixer(h, k_all, v_all, f_all, past, fox_w_in[j], q_norm[j], fox_w_out[j])
        x = x + gate * out
    return x, jnp.stack(new_s), k_new, v_new, logf_new


def setup_inputs(seed: int = 0) -> dict:
    key = jax.random.key(seed)
    ks = jax.random.split(key, 32)
    f32 = jnp.float32
    n_pages = PAST_LEN // PAGE_SIZE
    used = DEC_BATCH * n_pages
    n_pool = used + (used + 3) // 4
    nrm = lambda k, shape, s: jax.random.normal(k, shape, f32) * s
    page_table = jax.random.permutation(ks[0], n_pool)[:used].reshape(DEC_BATCH, n_pages).astype(jnp.int32)
    return {
        "x_prompt": nrm(ks[1], (BATCH, SEQ, D_MODEL), 1.0),
        "x_sample": nrm(ks[2], (DEC_BATCH, DEC_SEQ, D_MODEL), 1.0),
        "c_prompt": nrm(ks[3], (BATCH, D_MODEL), 1.0),
        "c_sample": nrm(ks[4], (DEC_BATCH, D_MODEL), 1.0),
        "state_gla": nrm(ks[5], (N_A, DEC_BATCH, H_A, DK_A, DV_A), 1.0),
        "cache_k": nrm(ks[6], (n_pool, PAGE_SIZE, H_B, HD_B), 1.0),
        "cache_v": nrm(ks[7], (n_pool, PAGE_SIZE, H_B, HD_B), 1.0),
        "cache_logf": jax.nn.log_sigmoid(3.5 + nrm(ks[8], (n_pool, PAGE_SIZE, H_B), 1.5)),
        "page_table": page_table,
        "ada_w": nrm(ks[9], (DEPTH, D_MODEL, 3 * D_MODEL), 0.5 * D_MODEL ** -0.5),
        "ada_b": nrm(ks[10], (DEPTH, 3 * D_MODEL), 0.02),
        "gla_w_in": nrm(ks[11], (N_A, D_MODEL, GLA_IN), D_MODEL ** -0.5),
        "gla_w_g2": nrm(ks[12], (N_A, GATE_RANK, DK_TOT), GATE_RANK ** -0.5),
        "gla_b_g": nrm(ks[13], (N_A, DK_TOT), 0.1),
        "gla_onorm": 1.0 + nrm(ks[14], (N_A, DV_A), 0.02),
        "gla_w_out": nrm(ks[15], (N_A, DV_TOT, D_MODEL), DV_TOT ** -0.5),
        "kv_ada_w": nrm(ks[16], (D_MODEL, 2 * D_MODEL), 0.5 * D_MODEL ** -0.5),
        "kv_ada_b": nrm(ks[17], (2 * D_MODEL,), 0.02),
        "w_kv": nrm(ks[18], (D_MODEL, KV_OUT), D_MODEL ** -0.5),
        "b_f": jax.random.uniform(ks[19], (H_B,), f32, 1.0, 6.0),
        "k_norm": 1.0 + nrm(ks[20], (HD_B,), 0.02),
        "fox_w_in": nrm(ks[21], (N_B, D_MODEL, 2 * D_B), D_MODEL ** -0.5),
        "q_norm": 1.0 + nrm(ks[22], (N_B, HD_B), 0.02),
        "fox_w_out": nrm(ks[23], (N_B, D_B, D_MODEL), D_B ** -0.5),
    }


def reference(x_prompt, x_sample, c_prompt, c_sample, state_gla, cache_k, cache_v, cache_logf, page_table,
              ada_w, ada_b, gla_w_in, gla_w_g2, gla_b_g, gla_onorm, gla_w_out, kv_ada_w, kv_ada_b, w_kv, b_f,
              k_norm, fox_w_in, q_norm, fox_w_out):
    bp = x_prompt.shape[0]
    bs = page_table.shape[0]
    s0 = jnp.zeros((N_A, bp, H_A, DK_A, DV_A), jnp.float32)
    k0 = jnp.zeros((bp, 0, H_B, HD_B), x_prompt.dtype)
    v0 = jnp.zeros((bp, 0, H_B, HD_B), x_prompt.dtype)
    lf0 = jnp.zeros((bp, 0, H_B), jnp.float32)
    y_prompt, s_p, k_p, v_p, lf_p = trunk(
        x_prompt, c_prompt, s0, k0, v0, lf0, GLA_CHUNK, ada_w, ada_b, gla_w_in, gla_w_g2, gla_b_g,
        gla_onorm, gla_w_out, kv_ada_w, kv_ada_b, w_kv, b_f, k_norm, fox_w_in, q_norm, fox_w_out)
    k_past = cache_k[page_table].reshape(bs, -1, H_B, HD_B)
    v_past = cache_v[page_table].reshape(bs, -1, H_B, HD_B)
    lf_past = cache_logf[page_table].reshape(bs, -1, H_B)
    y_sample, s_s, k_s, v_s, lf_s = trunk(
        x_sample, c_sample, state_gla, k_past, v_past, lf_past, 1, ada_w, ada_b, gla_w_in, gla_w_g2, gla_b_g,
        gla_onorm, gla_w_out, kv_ada_w, kv_ada_b, w_kv, b_f, k_norm, fox_w_in, q_norm, fox_w_out)
    return (y_prompt, y_sample, s_p, s_s, k_p, v_p, lf_p, k_s, v_s, lf_s)
```

```python
import functools

import jax
import jax.numpy as jnp
from jax import lax
from jax.experimental import pallas as pl
from jax.experimental.pallas import tpu as pltpu

F32 = jnp.float32
BF16 = jnp.bfloat16

EPS = 1e-6
GATE_TAU = 16.0
GLA_HEADS = 4
GLA_CHUNK = 64
GLA_SUB = 16
FOX_HD = 64
LANES = 128
NEG_BIG = -1e30
VMEM_LIMIT = 56 * 1024 * 1024

ROW_TILE = 512
ATTN_TQ = 512
ATTN_TK = 512
SCAN_T = 512
DEC_PAGES = 8
GLA_STEP_G = 8


def _params(sem):
    return pltpu.CompilerParams(dimension_semantics=sem, vmem_limit_bytes=VMEM_LIMIT)


def _silu(x):
    return x / (1.0 + jnp.exp(-x))


def _log_sigmoid(x):
    return jnp.minimum(x, 0.0) - jnp.log1p(jnp.exp(-jnp.abs(x)))


def _rms(x):
    return x * lax.rsqrt(jnp.mean(x * x, axis=-1, keepdims=True) + EPS)


def _dot(a, b):
    return jnp.dot(a, b, preferred_element_type=F32)


def _dot_nt(a, b):
    return lax.dot_general(a, b, (((1,), (1,)), ((), ())), preferred_element_type=F32)


def _dot_tn(a, b):
    return lax.dot_general(a, b, (((0,), (0,)), ((), ())), preferred_element_type=F32)


def _split3(x):
    p0 = x.astype(BF16)
    r1 = x - p0.astype(F32)
    p1 = r1.astype(BF16)
    p2 = (r1 - p1.astype(F32)).astype(BF16)
    return p0, p1, p2


def _exact_left_mul(mat01, x):
    p0, p1, p2 = _split3(x)
    return _dot(mat01, p0) + _dot(mat01, p1) + _dot(mat01, p2)


def _modulated_norm(x_ref, shift_ref, scale_ref):
    h = _rms(x_ref[...]) * (1.0 + scale_ref[...]) + shift_ref[...]
    return h.astype(BF16)


def _head_rms64(p):
    lane = lax.broadcasted_iota(jnp.int32, (1, LANES), 1)
    low = lane < FOX_HD
    outs = []
    for c in range(p.shape[-1] // LANES):
        blk = p[:, c * LANES:(c + 1) * LANES]
        sq = blk * blk
        s_lo = jnp.sum(jnp.where(low, sq, 0.0), axis=-1, keepdims=True)
        s_hi = jnp.sum(jnp.where(low, 0.0, sq), axis=-1, keepdims=True)
        r_lo = lax.rsqrt(s_lo * (1.0 / FOX_HD) + EPS)
        r_hi = lax.rsqrt(s_hi * (1.0 / FOX_HD) + EPS)
        outs.append(blk * jnp.where(low, r_lo, r_hi))
    return jnp.concatenate(outs, axis=-1)


def _ada_kernel(c_ref, w_ref, b_ref, o_ref):
    a = _silu(c_ref[...]).astype(BF16)
    o_ref[...] = _dot(a, w_ref[...].astype(BF16)) + b_ref[...]


def _ada(c_all, w, b, tn=1024):
    nl, d, n = w.shape
    rows = c_all.shape[0]
    return pl.pallas_call(
        _ada_kernel,
        grid=(nl, n // tn),
        in_specs=[
            pl.BlockSpec((rows, d), lambda l, j: (0, 0)),
            pl.BlockSpec((None, d, tn), lambda l, j: (l, 0, j)),
            pl.BlockSpec((None, 1, tn), lambda l, j: (l, 0, j)),
        ],
        out_specs=pl.BlockSpec((None, rows, tn), lambda l, j: (l, 0, j)),
        out_shape=jax.ShapeDtypeStruct((nl, rows, n), F32),
        compiler_params=_params(("parallel", "parallel")),
        name="ada_mod",
    )(c_all, w, b.reshape(nl, 1, n))


def _row_spec(tm, width):
    return pl.BlockSpec((None, tm, width), lambda b, i: (b, i, 0))


def _mod_spec(mod, tm):
    if mod.shape[1] == 1:
        return pl.BlockSpec((None, 1, mod.shape[2]), lambda b, i: (b, 0, 0))
    return pl.BlockSpec((None, tm, mod.shape[2]), lambda b, i: (b, i, 0))


def _full_spec(a):
    nd = a.ndim
    return pl.BlockSpec(a.shape, lambda b, i: (0,) * nd)


def _row_tile(length):
    return min(ROW_TILE, length)


def _gla_in_kernel(x_ref, sh_ref, sc_ref, wq_ref, wk_ref, wv_ref, wg_ref, wz_ref, wg2_ref, bg_ref,
                   q_ref, k_ref, v_ref, g_ref, la_ref, *, q_scale):
    h = _modulated_norm(x_ref, sh_ref, sc_ref)
    q_ref[...] = _dot(h, wq_ref[...]) * q_scale
    k_ref[...] = _dot(h, wk_ref[...])
    v_ref[...] = _dot(h, wv_ref[...])
    g_ref[...] = _dot(h, wg_ref[...])
    z = _dot(h, wz_ref[...])
    t = _dot(z.astype(BF16), wg2_ref[...]) + bg_ref[...]
    la_ref[...] = _log_sigmoid(t) * (1.0 / GATE_TAU)


def _gla_in(x, shift, scale, w_in, w_g2, b_g):
    bsz, length, d = x.shape
    dk_tot = w_g2.shape[1]
    rank = w_g2.shape[0]
    dv_tot = (w_in.shape[1] - 2 * dk_tot - rank) // 2
    dk = dk_tot // GLA_HEADS
    wb = w_in.astype(BF16)
    wq = wb[:, :dk_tot]
    wk = wb[:, dk_tot:2 * dk_tot]
    wv = wb[:, 2 * dk_tot:2 * dk_tot + dv_tot]
    wg = wb[:, 2 * dk_tot + dv_tot:2 * dk_tot + 2 * dv_tot]
    wz = wb[:, 2 * dk_tot + 2 * dv_tot:]
    wg2 = w_g2.astype(BF16)
    bg = b_g.reshape(1, dk_tot)
    tm = _row_tile(length)
    weights = (wq, wk, wv, wg, wz, wg2, bg)
    widths = (dk_tot, dk_tot, dv_tot, dv_tot, dk_tot)
    return pl.pallas_call(
        functools.partial(_gla_in_kernel, q_scale=dk ** -0.5),
        grid=(bsz, length // tm),
        in_specs=[_row_spec(tm, d), _mod_spec(shift, tm), _mod_spec(scale, tm)]
        + [_full_spec(w) for w in weights],
        out_specs=[_row_spec(tm, w) for w in widths],
        out_shape=[jax.ShapeDtypeStruct((bsz, length, w), F32) for w in widths],
        compiler_params=_params(("parallel", "parallel")),
        name="gla_in_proj",
    )(x, shift, scale, *weights)


def _gla_chunk_head(q, k, v, b, s_old, rows):
    c, dk = q.shape
    nsub = c // GLA_SUB
    b_last = b[c - 1:c, :]
    o = _dot((q * jnp.exp(b)).astype(BF16), s_old.astype(BF16))
    kd = k * jnp.exp(b_last - b)
    upd = _dot_tn(kd.astype(BF16), v.astype(BF16))
    decay_col = jnp.exp(jnp.transpose(jnp.broadcast_to(b_last, (dk, dk))))
    s_new = jnp.concatenate([decay_col] * (s_old.shape[1] // dk), axis=1) * s_old + upd
    qcat, kcat = [], []
    for j in range(nsub - 1):
        r_j = b[(j + 1) * GLA_SUB - 1:(j + 1) * GLA_SUB, :]
        in_j = (rows >= j * GLA_SUB) & (rows < (j + 1) * GLA_SUB)
        after_j = rows >= (j + 1) * GLA_SUB
        kcat.append(jnp.where(in_j, k * jnp.exp(jnp.where(in_j, r_j - b, 0.0)), 0.0))
        qcat.append(jnp.where(after_j, q * jnp.exp(jnp.where(after_j, b - r_j, 0.0)), 0.0))
    att = _dot_nt(jnp.concatenate(qcat, axis=1).astype(BF16), jnp.concatenate(kcat, axis=1).astype(BF16))
    sub_row = lax.broadcasted_iota(jnp.int32, (GLA_SUB, 1), 0)
    col_id = lax.broadcasted_iota(jnp.int32, (1, c), 1)
    diag_blocks = []
    for j in range(nsub):
        lo = j * GLA_SUB
        qj = q[lo:lo + GLA_SUB, :]
        bj = b[lo:lo + GLA_SUB, :]
        blk = jnp.zeros((GLA_SUB, c), F32)
        for t in range(GLA_SUB):
            src = lo + t
            valid = sub_row >= t
            e = jnp.exp(jnp.where(valid, bj - b[src:src + 1, :], 0.0))
            col = jnp.sum(qj * k[src:src + 1, :] * e, axis=-1, keepdims=True)
            blk = jnp.where((col_id == src) & valid, col, blk)
        diag_blocks.append(blk)
    att = att + jnp.concatenate(diag_blocks, axis=0)
    o = o + _dot(att.astype(BF16), v.astype(BF16))
    return o, s_new


def _gla_scan_kernel(q_ref, k_ref, v_ref, la_ref, o_ref, sfin_ref, s_sc, *, n_chunks):
    t = pl.program_id(1)
    dk = q_ref.shape[-1] // GLA_HEADS
    dv = v_ref.shape[-1] // GLA_HEADS
    c = GLA_CHUNK

    @pl.when(t == 0)
    def _():
        s_sc[...] = jnp.zeros_like(s_sc)

    tri = (lax.broadcasted_iota(jnp.int32, (c, c), 0) >= lax.broadcasted_iota(jnp.int32, (c, c), 1))
    tri = jnp.where(tri, 1.0, 0.0).astype(BF16)
    rows = lax.broadcasted_iota(jnp.int32, (c, 1), 0)

    def chunk(ci, carry):
        r0 = pl.multiple_of(ci * c, c)
        b_all = _exact_left_mul(tri, la_ref[pl.ds(r0, c), :])
        for h in range(GLA_HEADS):
            o, s_new = _gla_chunk_head(
                q_ref[pl.ds(r0, c), h * dk:(h + 1) * dk],
                k_ref[pl.ds(r0, c), h * dk:(h + 1) * dk],
                v_ref[pl.ds(r0, c), h * dv:(h + 1) * dv],
                b_all[:, h * dk:(h + 1) * dk],
                s_sc[h], rows)
            o_ref[pl.ds(r0, c), h * dv:(h + 1) * dv] = o
            s_sc[h] = s_new
        return carry

    lax.fori_loop(0, n_chunks, chunk, 0)

    @pl.when(t == pl.num_programs(1) - 1)
    def _():
        sfin_ref[...] = s_sc[...]


def _gla_scan(q, k, v, la):
    bsz, length, dk_tot = q.shape
    dv_tot = v.shape[-1]
    dk, dv = dk_tot // GLA_HEADS, dv_tot // GLA_HEADS
    tt = min(SCAN_T, length)
    return pl.pallas_call(
        functools.partial(_gla_scan_kernel, n_chunks=tt // GLA_CHUNK),
        grid=(bsz, length // tt),
        in_specs=[_row_spec(tt, dk_tot), _row_spec(tt, dk_tot), _row_spec(tt, dv_tot), _row_spec(tt, dk_tot)],
        out_specs=[_row_spec(tt, dv_tot),
                   pl.BlockSpec((None, GLA_HEADS, dk, dv), lambda b, i: (b, 0, 0, 0))],
        out_shape=[jax.ShapeDtypeStruct((bsz, length, dv_tot), F32),
                   jax.ShapeDtypeStruct((bsz, GLA_HEADS, dk, dv), F32)],
        scratch_shapes=[pltpu.VMEM((GLA_HEADS, dk, dv), F32)],
        compiler_params=_params(("parallel", "arbitrary")),
        name="gla_scan",
    )(q, k, v, la)


def _gla_step_kernel(qt_ref, kt_ref, lat_ref, v_ref, s_ref, o_ref, so_ref):
    dk, dv = s_ref.shape[-2:]
    for g in range(s_ref.shape[0]):
        for h in range(GLA_HEADS):
            q_col = qt_ref[h * dk:(h + 1) * dk, g:g + 1]
            k_col = kt_ref[h * dk:(h + 1) * dk, g:g + 1]
            a_col = jnp.exp(lat_ref[h * dk:(h + 1) * dk, g:g + 1])
            v_row = v_ref[g:g + 1, h * dv:(h + 1) * dv]
            s_new = a_col * s_ref[g, h] + k_col * v_row
            so_ref[g, h] = s_new
            o_ref[g:g + 1, h * dv:(h + 1) * dv] = jnp.sum(q_col * s_new, axis=0, keepdims=True)


def _gla_step(q, k, la, v, state):
    n, dk_tot = q.shape
    dv_tot = v.shape[-1]
    dk, dv = state.shape[-2:]
    g = GLA_STEP_G
    steps = n // g

    def cols(a):
        return a.reshape(steps, g, a.shape[-1]).transpose(0, 2, 1)

    col_spec = pl.BlockSpec((None, dk_tot, g), lambda i: (i, 0, 0))
    return pl.pallas_call(
        _gla_step_kernel,
        grid=(steps,),
        in_specs=[col_spec, col_spec, col_spec,
                  pl.BlockSpec((None, g, dv_tot), lambda i: (i, 0, 0)),
                  pl.BlockSpec((g, GLA_HEADS, dk, dv), lambda i: (i, 0, 0, 0))],
        out_specs=[pl.BlockSpec((None, g, dv_tot), lambda i: (i, 0, 0)),
                   pl.BlockSpec((g, GLA_HEADS, dk, dv), lambda i: (i, 0, 0, 0))],
        out_shape=[jax.ShapeDtypeStruct((steps, g, dv_tot), F32),
                   jax.ShapeDtypeStruct(state.shape, F32)],
        compiler_params=_params(("parallel",)),
        name="gla_step",
    )(cols(q), cols(k), cols(la), v.reshape(steps, g, dv_tot), state)


def _out_kernel(o_ref, g_ref, x_ref, gate_ref, w_ref, *rest, head_dim):
    if head_dim:
        onorm_ref, y_ref = rest
        o = o_ref[...]
        o = jnp.concatenate(
            [_rms(o[:, h * head_dim:(h + 1) * head_dim]) for h in range(o.shape[-1] // head_dim)], axis=-1)
        o = o * onorm_ref[...]
    else:
        (y_ref,) = rest
        o = o_ref[...]
    a = (o * _silu(g_ref[...])).astype(BF16)
    y_ref[...] = x_ref[...] + gate_ref[...] * _dot(a, w_ref[...])


def _out_proj(o, g, x, gate, w_out, onorm=None):
    bsz, length, d = x.shape
    width = o.shape[-1]
    tm = _row_tile(length)
    wb = w_out.astype(BF16)
    extra, head_dim = [], 0
    if onorm is not None:
        head_dim = onorm.shape[0]
        extra = [jnp.tile(onorm, width // head_dim).reshape(1, width)]
    return pl.pallas_call(
        functools.partial(_out_kernel, head_dim=head_dim),
        grid=(bsz, length // tm),
        in_specs=[_row_spec(tm, width), _row_spec(tm, width), _row_spec(tm, d), _mod_spec(gate, tm),
                  _full_spec(wb)] + [_full_spec(e) for e in extra],
        out_specs=_row_spec(tm, d),
        out_shape=jax.ShapeDtypeStruct((bsz, length, d), F32),
        compiler_params=_params(("parallel", "parallel")),
        name="out_proj",
    )(o, g, x, gate, wb, *extra)


def _kv_kernel(x_ref, sh_ref, sc_ref, wk_ref, wv_ref, wf_ref, bf_ref, kn_ref,
               k_ref, v_ref, lf_ref, kb_ref, vb_ref):
    h = _modulated_norm(x_ref, sh_ref, sc_ref)
    k = _head_rms64(_dot(h, wk_ref[...])) * kn_ref[...]
    v = _dot(h, wv_ref[...])
    k_ref[...] = k
    v_ref[...] = v
    kb_ref[...] = k.astype(BF16)
    vb_ref[...] = v.astype(BF16)
    lf_ref[...] = _log_sigmoid(_dot(h, wf_ref[...]) + bf_ref[...])


def _kv_proj(x, shift, scale, w_kv, b_f, k_norm):
    bsz, length, d = x.shape
    nh = b_f.shape[0]
    d_b = (w_kv.shape[1] - nh) // 2
    wb = w_kv.astype(BF16)
    weights = (wb[:, :d_b], wb[:, d_b:2 * d_b], wb[:, 2 * d_b:], b_f.reshape(1, nh),
               jnp.tile(k_norm, nh).reshape(1, d_b))
    tm = _row_tile(length)
    widths = ((d_b, F32), (d_b, F32), (nh, F32), (d_b, BF16), (d_b, BF16))
    return pl.pallas_call(
        _kv_kernel,
        grid=(bsz, length // tm),
        in_specs=[_row_spec(tm, d), _mod_spec(shift, tm), _mod_spec(scale, tm)]
        + [_full_spec(w) for w in weights],
        out_specs=[_row_spec(tm, w) for w, _ in widths],
        out_shape=[jax.ShapeDtypeStruct((bsz, length, w), dt) for w, dt in widths],
        compiler_params=_params(("parallel", "parallel")),
        name="kv_proj",
    )(x, shift, scale, *weights)


def _fox_in_kernel(x_ref, sh_ref, sc_ref, wq_ref, wg_ref, qn_ref, q_ref, g_ref):
    h = _modulated_norm(x_ref, sh_ref, sc_ref)
    q_ref[...] = (_head_rms64(_dot(h, wq_ref[...])) * qn_ref[...]).astype(BF16)
    g_ref[...] = _dot(h, wg_ref[...])


def _fox_in(x, shift, scale, w_in, q_norm):
    bsz, length, d = x.shape
    d_b = w_in.shape[1] // 2
    wb = w_in.astype(BF16)
    qn = (jnp.tile(q_norm, d_b // FOX_HD) * (FOX_HD ** -0.5)).reshape(1, d_b)
    weights = (wb[:, :d_b], wb[:, d_b:], qn)
    tm = _row_tile(length)
    return pl.pallas_call(
        _fox_in_kernel,
        grid=(bsz, length // tm),
        in_specs=[_row_spec(tm, d), _mod_spec(shift, tm), _mod_spec(scale, tm)]
        + [_full_spec(w) for w in weights],
        out_specs=[_row_spec(tm, d_b), _row_spec(tm, d_b)],
        out_shape=[jax.ShapeDtypeStruct((bsz, length, d_b), BF16),
                   jax.ShapeDtypeStruct((bsz, length, d_b), F32)],
        compiler_params=_params(("parallel", "parallel")),
        name="fox_in_proj",
    )(x, shift, scale, *weights)


def _cumsum_kernel(lf_ref, f_ref, carry_sc):
    r = lf_ref.shape[0]

    @pl.when(pl.program_id(1) == 0)
    def _():
        carry_sc[...] = jnp.zeros_like(carry_sc)

    tri = lax.broadcasted_iota(jnp.int32, (r, r), 0) >= lax.broadcasted_iota(jnp.int32, (r, r), 1)
    tri = jnp.where(tri, 1.0, 0.0).astype(BF16)
    f = _exact_left_mul(tri, lf_ref[...]) + carry_sc[...]
    f_ref[...] = f
    carry_sc[...] = f[r - 1:r, :]


def _cumsum_rows(lf):
    bsz, length, nh = lf.shape
    r = min(ROW_TILE, length)
    return pl.pallas_call(
        _cumsum_kernel,
        grid=(bsz, length // r),
        in_specs=[_row_spec(r, nh)],
        out_specs=_row_spec(r, nh),
        out_shape=jax.ShapeDtypeStruct(lf.shape, F32),
        scratch_shapes=[pltpu.VMEM((1, nh), F32)],
        compiler_params=_params(("parallel", "arbitrary")),
        name="forget_cumsum",
    )(lf)


def _fox_attn_kernel(q_ref, k_ref, v_ref, fq_ref, fk_ref, o_ref, m_sc, l_sc, acc_sc, *, tk):
    qi = pl.program_id(2)
    tq = q_ref.shape[0]
    lane = lax.broadcasted_iota(jnp.int32, (1, LANES), 1)
    low = lane < FOX_HD
    q = q_ref[...].astype(F32)
    q_heads = (jnp.where(low, q, 0.0).astype(BF16), jnp.where(low, 0.0, q).astype(BF16))
    m_sc[...] = jnp.full_like(m_sc, NEG_BIG)
    l_sc[...] = jnp.zeros_like(l_sc)
    acc_sc[...] = jnp.zeros_like(acc_sc)

    def block(kj, causal):
        c0 = pl.multiple_of(kj * tk, tk)
        ks = k_ref[pl.ds(c0, tk), :]
        vs = v_ref[pl.ds(c0, tk), :]
        for i in range(2):
            s = _dot_nt(q_heads[i], ks) + fq_ref[:, i:i + 1] - fk_ref[kj, i:i + 1, :]
            if causal:
                row = lax.broadcasted_iota(jnp.int32, (tq, tk), 0)
                col = lax.broadcasted_iota(jnp.int32, (tq, tk), 1)
                s = jnp.where(row >= col, s, NEG_BIG)
            m_old = m_sc[i]
            m_new = jnp.maximum(m_old, jnp.max(s, axis=-1, keepdims=True))
            alpha = jnp.exp(m_old - m_new)
            p = jnp.exp(s - m_new)
            l_sc[i] = alpha * l_sc[i] + jnp.sum(p, axis=-1, keepdims=True)
            acc_sc[i] = alpha * acc_sc[i] + _dot(p.astype(BF16), vs)
            m_sc[i] = m_new

    def body(kj, carry):
        block(kj, False)
        return carry

    lax.fori_loop(0, qi, body, 0)
    block(qi, True)
    o_ref[...] = jnp.where(low, acc_sc[0] / l_sc[0], acc_sc[1] / l_sc[1])


def _fox_attn(q, kb, vb, f):
    bsz, length, d_b = q.shape
    nh = f.shape[-1]
    pairs = nh // 2
    tq = tk = min(ATTN_TQ, length)
    fq = f.reshape(bsz, length, pairs, 2).transpose(0, 2, 1, 3)
    fk = fq.reshape(bsz, pairs, length // tk, tk, 2).transpose(0, 1, 2, 4, 3)
    return pl.pallas_call(
        functools.partial(_fox_attn_kernel, tk=tk),
        grid=(bsz, pairs, length // tq),
        in_specs=[
            pl.BlockSpec((None, tq, LANES), lambda b, p, i: (b, i, p)),
            pl.BlockSpec((None, length, LANES), lambda b, p, i: (b, 0, p)),
            pl.BlockSpec((None, length, LANES), lambda b, p, i: (b, 0, p)),
            pl.BlockSpec((None, None, tq, 2), lambda b, p, i: (b, p, i, 0)),
            pl.BlockSpec((None, None, length // tk, 2, tk), lambda b, p, i: (b, p, 0, 0, 0)),
        ],
        out_specs=pl.BlockSpec((None, tq, LANES), lambda b, p, i: (b, i, p)),
        out_shape=jax.ShapeDtypeStruct((bsz, length, d_b), F32),
        scratch_shapes=[pltpu.VMEM((2, tq, 1), F32), pltpu.VMEM((2, tq, 1), F32),
                        pltpu.VMEM((2, tq, LANES), F32)],
        compiler_params=_params(("parallel", "parallel", "arbitrary")),
        name="fox_attn",
    )(q, kb, vb, fq, fk)


def _suffix_kernel(pt_ref, *refs, n_pages):
    del pt_ref
    page_refs = refs[:n_pages]
    new_ref, d_ref = refs[n_pages:]
    ps = page_refs[0].shape[0]
    upper = lax.broadcasted_iota(jnp.int32, (ps, ps), 1) > lax.broadcasted_iota(jnp.int32, (ps, ps), 0)
    upper = jnp.where(upper, 1.0, 0.0).astype(BF16)
    carry = new_ref[...]
    for j in reversed(range(n_pages)):
        page = page_refs[j][...]
        d_ref[j * ps:(j + 1) * ps, :] = _exact_left_mul(upper, page) + carry
        carry = carry + jnp.sum(page, axis=0, keepdims=True)


def _forget_suffix(cache_logf, page_table, lf_new):
    n, n_pages = page_table.shape
    _, ps, nh = cache_logf.shape

    def page_spec(j):
        return pl.BlockSpec((None, ps, nh), lambda b, pt: (pt[b * n_pages + j], 0, 0))

    grid_spec = pltpu.PrefetchScalarGridSpec(
        num_scalar_prefetch=1,
        grid=(n,),
        in_specs=[page_spec(j) for j in range(n_pages)]
        + [pl.BlockSpec((None, 1, nh), lambda b, pt: (b, 0, 0))],
        out_specs=pl.BlockSpec((None, n_pages * ps, nh), lambda b, pt: (b, 0, 0)),
    )
    return pl.pallas_call(
        functools.partial(_suffix_kernel, n_pages=n_pages),
        grid_spec=grid_spec,
        out_shape=jax.ShapeDtypeStruct((n, n_pages * ps, nh), F32),
        compiler_params=_params(("arbitrary",)),
        name="forget_suffix",
    )(page_table.reshape(-1), *([cache_logf] * n_pages), lf_new)


def _decode_kernel(pt_ref, *refs, pages):
    del pt_ref
    k_pages = refs[:pages]
    v_pages = refs[pages:2 * pages]
    q_ref, kn_ref, vn_ref, d_ref, o_ref, m_sc, l_sc, acc_sc, kbuf, vbuf = refs[2 * pages:]
    step = pl.program_id(1)
    nh, d_b = acc_sc.shape
    ps = k_pages[0].shape[0]
    head_of_lane = lax.broadcasted_iota(jnp.int32, (nh, d_b), 1) // FOX_HD
    own = head_of_lane == lax.broadcasted_iota(jnp.int32, (nh, d_b), 0)
    q_rows = jnp.where(own, q_ref[...].astype(F32), 0.0).astype(BF16)

    @pl.when(step == 0)
    def _():
        k_new = kn_ref[...].astype(BF16).astype(F32)
        m_sc[...] = jnp.sum(q_rows.astype(F32) * k_new, axis=-1, keepdims=True)
        l_sc[...] = jnp.ones_like(l_sc)
        acc_sc[...] = jnp.broadcast_to(vn_ref[...], (nh, d_b))

    for j in range(pages):
        kbuf[j * ps:(j + 1) * ps, :] = k_pages[j][...].astype(BF16)
        vbuf[j * ps:(j + 1) * ps, :] = v_pages[j][...].astype(BF16)
    s = _dot_nt(q_rows, kbuf[...]) + d_ref[...]
    m_old = m_sc[...]
    m_new = jnp.maximum(m_old, jnp.max(s, axis=-1, keepdims=True))
    alpha = jnp.exp(m_old - m_new)
    p = jnp.exp(s - m_new)
    l_sc[...] = alpha * l_sc[...] + jnp.sum(p, axis=-1, keepdims=True)
    acc_sc[...] = alpha * acc_sc[...] + _dot(p.astype(BF16), vbuf[...])
    m_sc[...] = m_new

    @pl.when(step == pl.num_programs(1) - 1)
    def _():
        o_ref[...] = jnp.sum(jnp.where(own, acc_sc[...] / l_sc[...], 0.0), axis=0, keepdims=True)


def _decode_attn(q, k_new, v_new, bias_t, cache_k, cache_v, page_table):
    n, n_pages = page_table.shape
    _, ps, d_b = cache_k.shape
    nh = bias_t.shape[1]
    pages = min(DEC_PAGES, n_pages)
    steps = n_pages // pages

    def page_spec(j):
        return pl.BlockSpec((None, ps, d_b), lambda b, s, pt: (pt[b * n_pages + s * pages + j], 0, 0))

    row_spec = pl.BlockSpec((None, 1, d_b), lambda b, s, pt: (b, 0, 0))
    grid_spec = pltpu.PrefetchScalarGridSpec(
        num_scalar_prefetch=1,
        grid=(n, steps),
        in_specs=[page_spec(j) for j in range(pages)] + [page_spec(j) for j in range(pages)]
        + [row_spec, row_spec, row_spec,
           pl.BlockSpec((None, nh, pages * ps), lambda b, s, pt: (b, 0, s))],
        out_specs=row_spec,
        scratch_shapes=[pltpu.VMEM((nh, 1), F32), pltpu.VMEM((nh, 1), F32), pltpu.VMEM((nh, d_b), F32),
                        pltpu.VMEM((pages * ps, d_b), BF16), pltpu.VMEM((pages * ps, d_b), BF16)],
    )
    return pl.pallas_call(
        functools.partial(_decode_kernel, pages=pages),
        grid_spec=grid_spec,
        out_shape=jax.ShapeDtypeStruct((n, 1, d_b), F32),
        compiler_params=_params(("parallel", "arbitrary")),
        name="fox_decode_attn",
    )(page_table.reshape(-1), *([cache_k] * pages), *([cache_v] * pages), q, k_new, v_new, bias_t)


def kernel(x_prompt, x_sample, c_prompt, c_sample, state_gla, cache_k, cache_v, cache_logf, page_table,
           ada_w, ada_b, gla_w_in, gla_w_g2, gla_b_g, gla_onorm, gla_w_out, kv_ada_w, kv_ada_b, w_kv, b_f,
           k_norm, fox_w_in, q_norm, fox_w_out):
    bp, seq, d = x_prompt.shape
    ns = x_sample.shape[0]
    n_a = gla_w_in.shape[0]
    depth = ada_w.shape[0]
    nh = b_f.shape[0]
    d_b = nh * FOX_HD
    dk, dv = state_gla.shape[-2:]

    pad = (-(ns + bp)) % 8
    c_all = jnp.concatenate([c_sample, c_prompt, jnp.zeros((pad, d), F32)], axis=0)
    mod = _ada(c_all, ada_w, ada_b)
    kv_mod = _ada(c_all, kv_ada_w[None], kv_ada_b[None])[0]

    def split_mod(m, n_parts):
        parts = jnp.split(m, n_parts, axis=-1)
        return ([p[ns:ns + bp, None, :] for p in parts],
                [p[None, :ns, :] for p in parts])

    xp = x_prompt
    xs = x_sample.reshape(1, ns, d)
    sp_states, ss_states = [], []
    for layer in range(n_a):
        mod_p, mod_s = split_mod(mod[layer], 3)
        q, k, v, g, la = _gla_in(xp, mod_p[0], mod_p[1], gla_w_in[layer], gla_w_g2[layer], gla_b_g[layer])
        o, s_fin = _gla_scan(q, k, v, la)
        sp_states.append(s_fin)
        xp = _out_proj(o, g, xp, mod_p[2], gla_w_out[layer], gla_onorm[layer])
        q, k, v, g, la = _gla_in(xs, mod_s[0], mod_s[1], gla_w_in[layer], gla_w_g2[layer], gla_b_g[layer])
        o, s_new = _gla_step(q[0], k[0], la[0], v[0], state_gla[layer])
        ss_states.append(s_new)
        xs = _out_proj(o.reshape(1, ns, -1), g, xs, mod_s[2], gla_w_out[layer], gla_onorm[layer])

    kvm_p, kvm_s = split_mod(kv_mod, 2)
    k_p, v_p, lf_p, kb_p, vb_p = _kv_proj(xp, kvm_p[0], kvm_p[1], w_kv, b_f, k_norm)
    k_s, v_s, lf_s, _, _ = _kv_proj(xs, kvm_s[0], kvm_s[1], w_kv, b_f, k_norm)
    f_p = _cumsum_rows(lf_p)
    lf_new = lf_s.reshape(ns, 1, nh)
    bias_t = jnp.swapaxes(_forget_suffix(cache_logf, page_table, lf_new), 1, 2)
    ck = cache_k.reshape(cache_k.shape[0], cache_k.shape[1], d_b)
    cv = cache_v.reshape(cache_v.shape[0], cache_v.shape[1], d_b)
    k_new = k_s.reshape(ns, 1, d_b)
    v_new = v_s.reshape(ns, 1, d_b)

    for j in range(depth - n_a):
        mod_p, mod_s = split_mod(mod[n_a + j], 3)
        q, g = _fox_in(xp, mod_p[0], mod_p[1], fox_w_in[j], q_norm[j])
        o = _fox_attn(q, kb_p, vb_p, f_p)
        xp = _out_proj(o, g, xp, mod_p[2], fox_w_out[j])
        q, g = _fox_in(xs, mod_s[0], mod_s[1], fox_w_in[j], q_norm[j])
        o = _decode_attn(q.reshape(ns, 1, d_b), k_new, v_new, bias_t, ck, cv, page_table)
        xs = _out_proj(o.reshape(1, ns, d_b), g, xs, mod_s[2], fox_w_out[j])

    return (xp, xs.reshape(ns, 1, d),
            jnp.stack(sp_states), jnp.stack(ss_states),
            k_p.reshape(bp, seq, nh, FOX_HD), v_p.reshape(bp, seq, nh, FOX_HD), lf_p,
            k_new.reshape(ns, 1, nh, FOX_HD), v_new.reshape(ns, 1, nh, FOX_HD), lf_new)
```

```python
import functools

import jax
import jax.numpy as jnp
from jax import lax
from jax.experimental import pallas as pl
from jax.experimental.pallas import tpu as pltpu

F32 = jnp.float32
BF16 = jnp.bfloat16

EPS = 1e-6
GATE_TAU = 16.0
GLA_HEADS = 4
GLA_CHUNK = 64
GLA_SUB = 16
FOX_HD = 64
LANES = 128
NEG_BIG = -1e30
LOG2E = 1.4426950408889634
VMEM_LIMIT = 56 * 1024 * 1024

ROW_TILE = 512
ATTN_TQ = 512
ATTN_TK = 512
SCAN_T = 512
DEC_PAGES = 16
GLA_STEP_G = 8


def _params(sem):
    return pltpu.CompilerParams(dimension_semantics=sem, vmem_limit_bytes=VMEM_LIMIT)


def _silu(x):
    return x / (1.0 + jnp.exp(-x))


def _log_sigmoid(x):
    return jnp.minimum(x, 0.0) - jnp.log1p(jnp.exp(-jnp.abs(x)))


def _rms(x):
    return x * lax.rsqrt(jnp.mean(x * x, axis=-1, keepdims=True) + EPS)


def _dot(a, b):
    return jnp.dot(a, b, preferred_element_type=F32)


def _dot_nt(a, b):
    return lax.dot_general(a, b, (((1,), (1,)), ((), ())), preferred_element_type=F32)


def _dot_tn(a, b):
    return lax.dot_general(a, b, (((0,), (0,)), ((), ())), preferred_element_type=F32)


def _split3(x):
    p0 = x.astype(BF16)
    r1 = x - p0.astype(F32)
    p1 = r1.astype(BF16)
    p2 = (r1 - p1.astype(F32)).astype(BF16)
    return p0, p1, p2


def _exact_left_mul(mat01, x):
    p0, p1, p2 = _split3(x)
    return _dot(mat01, p0) + _dot(mat01, p1) + _dot(mat01, p2)


def _modulated_norm(x_ref, shift_ref, scale_ref):
    h = _rms(x_ref[...]) * (1.0 + scale_ref[...]) + shift_ref[...]
    return h.astype(BF16)


def _head_rms64(p):
    lane = lax.broadcasted_iota(jnp.int32, (1, LANES), 1)
    low = lane < FOX_HD
    outs = []
    for c in range(p.shape[-1] // LANES):
        blk = p[:, c * LANES:(c + 1) * LANES]
        sq = blk * blk
        s_lo = jnp.sum(jnp.where(low, sq, 0.0), axis=-1, keepdims=True)
        s_hi = jnp.sum(jnp.where(low, 0.0, sq), axis=-1, keepdims=True)
        r_lo = lax.rsqrt(s_lo * (1.0 / FOX_HD) + EPS)
        r_hi = lax.rsqrt(s_hi * (1.0 / FOX_HD) + EPS)
        outs.append(blk * jnp.where(low, r_lo, r_hi))
    return jnp.concatenate(outs, axis=-1)


def _ada_kernel(c_ref, w_ref, b_ref, o_ref):
    a = _silu(c_ref[...]).astype(BF16)
    o_ref[...] = _dot(a, w_ref[...].astype(BF16)) + b_ref[...]


def _ada(c_all, w, b, tn=1024):
    nl, d, n = w.shape
    rows = c_all.shape[0]
    return pl.pallas_call(
        _ada_kernel,
        grid=(nl, n // tn),
        in_specs=[
            pl.BlockSpec((rows, d), lambda l, j: (0, 0)),
            pl.BlockSpec((None, d, tn), lambda l, j: (l, 0, j)),
            pl.BlockSpec((None, 1, tn), lambda l, j: (l, 0, j)),
        ],
        out_specs=pl.BlockSpec((None, rows, tn), lambda l, j: (l, 0, j)),
        out_shape=jax.ShapeDtypeStruct((nl, rows, n), F32),
        compiler_params=_params(("parallel", "parallel")),
        name="ada_mod",
    )(c_all, w, b.reshape(nl, 1, n))


def _row_spec(tm, width):
    return pl.BlockSpec((None, tm, width), lambda b, i: (b, i, 0))


def _mod_spec(mod, tm):
    if mod.shape[1] == 1:
        return pl.BlockSpec((None, 1, mod.shape[2]), lambda b, i: (b, 0, 0))
    return pl.BlockSpec((None, tm, mod.shape[2]), lambda b, i: (b, i, 0))


def _full_spec(a):
    nd = a.ndim
    return pl.BlockSpec(a.shape, lambda b, i: (0,) * nd)


def _row_tile(length):
    return min(ROW_TILE, length)


def _gla_in_kernel(x_ref, sh_ref, sc_ref, wq_ref, wk_ref, wv_ref, wg_ref, wz_ref, wg2_ref, bg_ref,
                   q_ref, k_ref, v_ref, g_ref, la_ref, *, q_scale):
    h = _modulated_norm(x_ref, sh_ref, sc_ref)
    q_ref[...] = _dot(h, wq_ref[...]) * q_scale
    k_ref[...] = _dot(h, wk_ref[...])
    v_ref[...] = _dot(h, wv_ref[...])
    g_ref[...] = _dot(h, wg_ref[...])
    z = _dot(h, wz_ref[...])
    t = _dot(z.astype(BF16), wg2_ref[...]) + bg_ref[...]
    la_ref[...] = _log_sigmoid(t) * (1.0 / GATE_TAU)


def _gla_in(x, shift, scale, w_in, w_g2, b_g):
    bsz, length, d = x.shape
    dk_tot = w_g2.shape[1]
    rank = w_g2.shape[0]
    dv_tot = (w_in.shape[1] - 2 * dk_tot - rank) // 2
    dk = dk_tot // GLA_HEADS
    wb = w_in.astype(BF16)
    wq = wb[:, :dk_tot]
    wk = wb[:, dk_tot:2 * dk_tot]
    wv = wb[:, 2 * dk_tot:2 * dk_tot + dv_tot]
    wg = wb[:, 2 * dk_tot + dv_tot:2 * dk_tot + 2 * dv_tot]
    wz = wb[:, 2 * dk_tot + 2 * dv_tot:]
    wg2 = w_g2.astype(BF16)
    bg = b_g.reshape(1, dk_tot)
    tm = _row_tile(length)
    weights = (wq, wk, wv, wg, wz, wg2, bg)
    widths = (dk_tot, dk_tot, dv_tot, dv_tot, dk_tot)
    return pl.pallas_call(
        functools.partial(_gla_in_kernel, q_scale=dk ** -0.5),
        grid=(bsz, length // tm),
        in_specs=[_row_spec(tm, d), _mod_spec(shift, tm), _mod_spec(scale, tm)]
        + [_full_spec(w) for w in weights],
        out_specs=[_row_spec(tm, w) for w in widths],
        out_shape=[jax.ShapeDtypeStruct((bsz, length, w), F32) for w in widths],
        compiler_params=_params(("parallel", "parallel")),
        name="gla_in_proj",
    )(x, shift, scale, *weights)


def _gla_chunk_head(q, k, v, b, s_old, rows):
    c, dk = q.shape
    nsub = c // GLA_SUB
    b_last = b[c - 1:c, :]
    o = _dot((q * jnp.exp(b)).astype(BF16), s_old.astype(BF16))
    kd = k * jnp.exp(b_last - b)
    upd = _dot_tn(kd.astype(BF16), v.astype(BF16))
    decay_col = jnp.exp(jnp.transpose(jnp.broadcast_to(b_last, (dk, dk))))
    s_new = jnp.concatenate([decay_col] * (s_old.shape[1] // dk), axis=1) * s_old + upd
    qcat, kcat = [], []
    for j in range(nsub - 1):
        r_j = b[(j + 1) * GLA_SUB - 1:(j + 1) * GLA_SUB, :]
        in_j = (rows >= j * GLA_SUB) & (rows < (j + 1) * GLA_SUB)
        after_j = rows >= (j + 1) * GLA_SUB
        kcat.append(jnp.where(in_j, k * jnp.exp(jnp.where(in_j, r_j - b, 0.0)), 0.0))
        qcat.append(jnp.where(after_j, q * jnp.exp(jnp.where(after_j, b - r_j, 0.0)), 0.0))
    att = _dot_nt(jnp.concatenate(qcat, axis=1).astype(BF16), jnp.concatenate(kcat, axis=1).astype(BF16))
    sub_row = lax.broadcasted_iota(jnp.int32, (GLA_SUB, 1), 0)
    col_id = lax.broadcasted_iota(jnp.int32, (1, c), 1)
    diag_blocks = []
    for j in range(nsub):
        lo = j * GLA_SUB
        qj = q[lo:lo + GLA_SUB, :]
        bj = b[lo:lo + GLA_SUB, :]
        blk = jnp.zeros((GLA_SUB, c), F32)
        for t in range(GLA_SUB):
            src = lo + t
            valid = sub_row >= t
            e = jnp.exp(jnp.where(valid, bj - b[src:src + 1, :], 0.0))
            col = jnp.sum(qj * k[src:src + 1, :] * e, axis=-1, keepdims=True)
            blk = jnp.where((col_id == src) & valid, col, blk)
        diag_blocks.append(blk)
    att = att + jnp.concatenate(diag_blocks, axis=0)
    o = o + _dot(att.astype(BF16), v.astype(BF16))
    return o, s_new


def _gla_scan_kernel(q_ref, k_ref, v_ref, la_ref, o_ref, sfin_ref, s_sc, *, n_chunks):
    t = pl.program_id(1)
    dk = q_ref.shape[-1] // GLA_HEADS
    dv = v_ref.shape[-1] // GLA_HEADS
    c = GLA_CHUNK

    @pl.when(t == 0)
    def _():
        s_sc[...] = jnp.zeros_like(s_sc)

    tri = (lax.broadcasted_iota(jnp.int32, (c, c), 0) >= lax.broadcasted_iota(jnp.int32, (c, c), 1))
    tri = jnp.where(tri, 1.0, 0.0).astype(BF16)
    rows = lax.broadcasted_iota(jnp.int32, (c, 1), 0)

    def chunk(ci, carry):
        r0 = pl.multiple_of(ci * c, c)
        b_all = _exact_left_mul(tri, la_ref[pl.ds(r0, c), :])
        for h in range(GLA_HEADS):
            o, s_new = _gla_chunk_head(
                q_ref[pl.ds(r0, c), h * dk:(h + 1) * dk],
                k_ref[pl.ds(r0, c), h * dk:(h + 1) * dk],
                v_ref[pl.ds(r0, c), h * dv:(h + 1) * dv],
                b_all[:, h * dk:(h + 1) * dk],
                s_sc[h], rows)
            o_ref[pl.ds(r0, c), h * dv:(h + 1) * dv] = o
            s_sc[h] = s_new
        return carry

    lax.fori_loop(0, n_chunks, chunk, 0)

    @pl.when(t == pl.num_programs(1) - 1)
    def _():
        sfin_ref[...] = s_sc[...]


def _gla_scan(q, k, v, la):
    bsz, length, dk_tot = q.shape
    dv_tot = v.shape[-1]
    dk, dv = dk_tot // GLA_HEADS, dv_tot // GLA_HEADS
    tt = min(SCAN_T, length)
    return pl.pallas_call(
        functools.partial(_gla_scan_kernel, n_chunks=tt // GLA_CHUNK),
        grid=(bsz, length // tt),
        in_specs=[_row_spec(tt, dk_tot), _row_spec(tt, dk_tot), _row_spec(tt, dv_tot), _row_spec(tt, dk_tot)],
        out_specs=[_row_spec(tt, dv_tot),
                   pl.BlockSpec((None, GLA_HEADS, dk, dv), lambda b, i: (b, 0, 0, 0))],
        out_shape=[jax.ShapeDtypeStruct((bsz, length, dv_tot), F32),
                   jax.ShapeDtypeStruct((bsz, GLA_HEADS, dk, dv), F32)],
        scratch_shapes=[pltpu.VMEM((GLA_HEADS, dk, dv), F32)],
        compiler_params=_params(("parallel", "arbitrary")),
        name="gla_scan",
    )(q, k, v, la)


def _gla_step_kernel(qt_ref, kt_ref, lat_ref, v_ref, s_ref, o_ref, so_ref):
    dk, dv = s_ref.shape[-2:]
    for g in range(s_ref.shape[0]):
        for h in range(GLA_HEADS):
            q_col = qt_ref[h * dk:(h + 1) * dk, g:g + 1]
            k_col = kt_ref[h * dk:(h + 1) * dk, g:g + 1]
            a_col = jnp.exp(lat_ref[h * dk:(h + 1) * dk, g:g + 1])
            v_row = v_ref[g:g + 1, h * dv:(h + 1) * dv]
            s_new = a_col * s_ref[g, h] + k_col * v_row
            so_ref[g, h] = s_new
            o_ref[g:g + 1, h * dv:(h + 1) * dv] = jnp.sum(q_col * s_new, axis=0, keepdims=True)


def _gla_step(q, k, la, v, state):
    n, dk_tot = q.shape
    dv_tot = v.shape[-1]
    dk, dv = state.shape[-2:]
    g = GLA_STEP_G
    steps = n // g

    def cols(a):
        return a.reshape(steps, g, a.shape[-1]).transpose(0, 2, 1)

    col_spec = pl.BlockSpec((None, dk_tot, g), lambda i: (i, 0, 0))
    return pl.pallas_call(
        _gla_step_kernel,
        grid=(steps,),
        in_specs=[col_spec, col_spec, col_spec,
                  pl.BlockSpec((None, g, dv_tot), lambda i: (i, 0, 0)),
                  pl.BlockSpec((g, GLA_HEADS, dk, dv), lambda i: (i, 0, 0, 0))],
        out_specs=[pl.BlockSpec((None, g, dv_tot), lambda i: (i, 0, 0)),
                   pl.BlockSpec((g, GLA_HEADS, dk, dv), lambda i: (i, 0, 0, 0))],
        out_shape=[jax.ShapeDtypeStruct((steps, g, dv_tot), F32),
                   jax.ShapeDtypeStruct(state.shape, F32)],
        compiler_params=_params(("parallel",)),
        name="gla_step",
    )(cols(q), cols(k), cols(la), v.reshape(steps, g, dv_tot), state)


def _out_kernel(o_ref, g_ref, x_ref, gate_ref, w_ref, *rest, head_dim):
    if head_dim:
        onorm_ref, y_ref = rest
        o = o_ref[...]
        o = jnp.concatenate(
            [_rms(o[:, h * head_dim:(h + 1) * head_dim]) for h in range(o.shape[-1] // head_dim)], axis=-1)
        o = o * onorm_ref[...]
    else:
        (y_ref,) = rest
        o = o_ref[...]
    a = (o * _silu(g_ref[...])).astype(BF16)
    y_ref[...] = x_ref[...] + gate_ref[...] * _dot(a, w_ref[...])


def _out_proj(o, g, x, gate, w_out, onorm=None):
    bsz, length, d = x.shape
    width = o.shape[-1]
    tm = _row_tile(length)
    wb = w_out.astype(BF16)
    extra, head_dim = [], 0
    if onorm is not None:
        head_dim = onorm.shape[0]
        extra = [jnp.tile(onorm, width // head_dim).reshape(1, width)]
    return pl.pallas_call(
        functools.partial(_out_kernel, head_dim=head_dim),
        grid=(bsz, length // tm),
        in_specs=[_row_spec(tm, width), _row_spec(tm, width), _row_spec(tm, d), _mod_spec(gate, tm),
                  _full_spec(wb)] + [_full_spec(e) for e in extra],
        out_specs=_row_spec(tm, d),
        out_shape=jax.ShapeDtypeStruct((bsz, length, d), F32),
        compiler_params=_params(("parallel", "parallel")),
        name="out_proj",
    )(o, g, x, gate, wb, *extra)


def _kv_kernel(x_ref, sh_ref, sc_ref, wk_ref, wv_ref, wf_ref, bf_ref, kn_ref,
               k_ref, v_ref, lf_ref, kb_ref, vb_ref):
    h = _modulated_norm(x_ref, sh_ref, sc_ref)
    k = _head_rms64(_dot(h, wk_ref[...])) * kn_ref[...]
    v = _dot(h, wv_ref[...])
    k_ref[...] = k
    v_ref[...] = v
    kb_ref[...] = k.astype(BF16)
    vb_ref[...] = v.astype(BF16)
    lf_ref[...] = _log_sigmoid(_dot(h, wf_ref[...]) + bf_ref[...])


def _kv_proj(x, shift, scale, w_kv, b_f, k_norm):
    bsz, length, d = x.shape
    nh = b_f.shape[0]
    d_b = (w_kv.shape[1] - nh) // 2
    wb = w_kv.astype(BF16)
    weights = (wb[:, :d_b], wb[:, d_b:2 * d_b], wb[:, 2 * d_b:], b_f.reshape(1, nh),
               jnp.tile(k_norm, nh).reshape(1, d_b))
    tm = _row_tile(length)
    widths = ((d_b, F32), (d_b, F32), (nh, F32), (d_b, BF16), (d_b, BF16))
    return pl.pallas_call(
        _kv_kernel,
        grid=(bsz, length // tm),
        in_specs=[_row_spec(tm, d), _mod_spec(shift, tm), _mod_spec(scale, tm)]
        + [_full_spec(w) for w in weights],
        out_specs=[_row_spec(tm, w) for w, _ in widths],
        out_shape=[jax.ShapeDtypeStruct((bsz, length, w), dt) for w, dt in widths],
        compiler_params=_params(("parallel", "parallel")),
        name="kv_proj",
    )(x, shift, scale, *weights)


def _fox_in_kernel(x_ref, sh_ref, sc_ref, wq_ref, wg_ref, qn_ref, q_ref, g_ref):
    h = _modulated_norm(x_ref, sh_ref, sc_ref)
    q_ref[...] = (_head_rms64(_dot(h, wq_ref[...])) * qn_ref[...]).astype(BF16)
    g_ref[...] = _dot(h, wg_ref[...])


def _fox_in(x, shift, scale, w_in, q_norm):
    bsz, length, d = x.shape
    d_b = w_in.shape[1] // 2
    wb = w_in.astype(BF16)
    qn = (jnp.tile(q_norm, d_b // FOX_HD) * (FOX_HD ** -0.5 * LOG2E)).reshape(1, d_b)
    weights = (wb[:, :d_b], wb[:, d_b:], qn)
    tm = _row_tile(length)
    return pl.pallas_call(
        _fox_in_kernel,
        grid=(bsz, length // tm),
        in_specs=[_row_spec(tm, d), _mod_spec(shift, tm), _mod_spec(scale, tm)]
        + [_full_spec(w) for w in weights],
        out_specs=[_row_spec(tm, d_b), _row_spec(tm, d_b)],
        out_shape=[jax.ShapeDtypeStruct((bsz, length, d_b), BF16),
                   jax.ShapeDtypeStruct((bsz, length, d_b), F32)],
        compiler_params=_params(("parallel", "parallel")),
        name="fox_in_proj",
    )(x, shift, scale, *weights)


def _cumsum_kernel(lf_ref, f_ref, carry_sc):
    r = lf_ref.shape[0]

    @pl.when(pl.program_id(1) == 0)
    def _():
        carry_sc[...] = jnp.zeros_like(carry_sc)

    tri = lax.broadcasted_iota(jnp.int32, (r, r), 0) >= lax.broadcasted_iota(jnp.int32, (r, r), 1)
    tri = jnp.where(tri, 1.0, 0.0).astype(BF16)
    f = _exact_left_mul(tri, lf_ref[...]) + carry_sc[...]
    f_ref[...] = f * LOG2E
    carry_sc[...] = f[r - 1:r, :]


def _cumsum_rows(lf):
    bsz, length, nh = lf.shape
    r = min(ROW_TILE, length)
    return pl.pallas_call(
        _cumsum_kernel,
        grid=(bsz, length // r),
        in_specs=[_row_spec(r, nh)],
        out_specs=_row_spec(r, nh),
        out_shape=jax.ShapeDtypeStruct(lf.shape, F32),
        scratch_shapes=[pltpu.VMEM((1, nh), F32)],
        compiler_params=_params(("parallel", "arbitrary")),
        name="forget_cumsum",
    )(lf)


def _fox_attn_kernel(q_ref, k_ref, vt_ref, fq_ref, fk_ref, o_ref, fkb_sc, m_sc, l_sc, acc_sc, *, tk):
    qi = pl.program_id(2)
    tq = q_ref.shape[0]
    length = k_ref.shape[0]
    lane = lax.broadcasted_iota(jnp.int32, (1, LANES), 1)
    low = lane < FOX_HD
    q = q_ref[...].astype(F32)
    q_heads = (jnp.where(low, q, 0.0).astype(BF16), jnp.where(low, 0.0, q).astype(BF16))

    @pl.when(qi == 0)
    def _():
        def fill(c, carry):
            r0 = pl.multiple_of(c * tk, tk)
            for i in range(2):
                fkb_sc[i, pl.ds(r0, tk), :] = jnp.broadcast_to(fk_ref[pl.ds(r0, tk), i:i + 1], (tk, LANES))
            return carry
        lax.fori_loop(0, length // tk, fill, 0)

    m_sc[...] = jnp.full_like(m_sc, NEG_BIG)
    l_sc[...] = jnp.zeros_like(l_sc)
    acc_sc[...] = jnp.zeros_like(acc_sc)

    def block(kj, qlo):
        causal = qlo is not None
        qlo = qlo or 0
        nq = tq - qlo
        c0 = pl.multiple_of(kj * tk, tk)
        ks = k_ref[pl.ds(c0, tk), :]
        ts = []
        for i in range(2):
            t = _dot_nt(ks, q_heads[i][qlo:, :]) - jnp.concatenate(
                [fkb_sc[i, pl.ds(c0, tk), :]] * (nq // LANES), axis=1)
            if causal:
                key = lax.broadcasted_iota(jnp.int32, (tk, nq), 0)
                qry = lax.broadcasted_iota(jnp.int32, (tk, nq), 1)
                t = jnp.where(key <= qry, t, NEG_BIG)
            ts.append(t)
        ps, alphas = [], []
        for i in range(2):
            fq = fq_ref[i:i + 1, qlo:]
            m_old = m_sc[i, :, qlo:]
            m_new = jnp.maximum(m_old, jnp.max(ts[i], axis=0, keepdims=True) + fq)
            alpha = jnp.exp2(m_old - m_new)
            p = jnp.exp2(ts[i] + (fq - m_new))
            l_sc[i, :, qlo:] = alpha * l_sc[i, :, qlo:] + jnp.sum(p, axis=0, keepdims=True)
            m_sc[i, :, qlo:] = m_new
            ps.append(p.astype(BF16))
            alphas.append(alpha)
        for i in range(2):
            vt = vt_ref[kj, i * FOX_HD:(i + 1) * FOX_HD, :]
            acc_sc[i, :, qlo:] = alphas[i] * acc_sc[i, :, qlo:] + _dot(vt, ps[i])

    def body(kj, carry):
        block(kj, None)
        return carry

    ratio = tq // tk
    lax.fori_loop(0, qi * ratio, body, 0)
    for d in range(ratio):
        block(qi * ratio + d, d * tk)
    o_t = jnp.concatenate([acc_sc[0] / l_sc[0], acc_sc[1] / l_sc[1]], axis=0)
    o_ref[...] = jnp.transpose(o_t)


def _fox_attn_layouts(vb, f2, tk):
    bsz, length, d_b = vb.shape
    pairs = f2.shape[-1] // 2
    fpair = f2.reshape(bsz, length, pairs, 2)
    fq = fpair.transpose(0, 2, 3, 1)
    fk = fpair.transpose(0, 2, 1, 3)
    vt = vb.reshape(bsz, length // tk, tk, pairs, LANES).transpose(0, 3, 1, 4, 2)
    return vt, fq, fk


def _fox_attn(q, kb, vt, fq, fk):
    bsz, length, d_b = q.shape
    pairs = fq.shape[1]
    nk, _, tk = vt.shape[2:]
    tq = max(tk, min(ATTN_TQ, length))
    return pl.pallas_call(
        functools.partial(_fox_attn_kernel, tk=tk),
        grid=(bsz, pairs, length // tq),
        in_specs=[
            pl.BlockSpec((None, tq, LANES), lambda b, p, i: (b, i, p)),
            pl.BlockSpec((None, length, LANES), lambda b, p, i: (b, 0, p)),
            pl.BlockSpec((None, None, nk, LANES, tk), lambda b, p, i: (b, p, 0, 0, 0)),
            pl.BlockSpec((None, None, 2, tq), lambda b, p, i: (b, p, 0, i)),
            pl.BlockSpec((None, None, length, 2), lambda b, p, i: (b, p, 0, 0)),
        ],
        out_specs=pl.BlockSpec((None, tq, LANES), lambda b, p, i: (b, i, p)),
        out_shape=jax.ShapeDtypeStruct((bsz, length, d_b), F32),
        scratch_shapes=[pltpu.VMEM((2, length, LANES), F32),
                        pltpu.VMEM((2, 1, tq), F32), pltpu.VMEM((2, 1, tq), F32),
                        pltpu.VMEM((2, FOX_HD, tq), F32)],
        compiler_params=_params(("parallel", "parallel", "arbitrary")),
        name="fox_attn",
    )(q, kb, vt, fq, fk)


def _suffix_kernel(pt_ref, *refs, n_pages):
    del pt_ref
    page_refs = refs[:n_pages]
    new_ref, d_ref = refs[n_pages:]
    ps = page_refs[0].shape[0]
    upper = lax.broadcasted_iota(jnp.int32, (ps, ps), 1) > lax.broadcasted_iota(jnp.int32, (ps, ps), 0)
    upper = jnp.where(upper, 1.0, 0.0).astype(BF16)
    carry = new_ref[...]
    for j in reversed(range(n_pages)):
        page = page_refs[j][...]
        d_ref[j * ps:(j + 1) * ps, :] = (_exact_left_mul(upper, page) + carry) * LOG2E
        carry = carry + jnp.sum(page, axis=0, keepdims=True)


def _forget_suffix(cache_logf, page_table, lf_new):
    n, n_pages = page_table.shape
    _, ps, nh = cache_logf.shape

    def page_spec(j):
        return pl.BlockSpec((None, ps, nh), lambda b, pt: (pt[b * n_pages + j], 0, 0))

    grid_spec = pltpu.PrefetchScalarGridSpec(
        num_scalar_prefetch=1,
        grid=(n,),
        in_specs=[page_spec(j) for j in range(n_pages)]
        + [pl.BlockSpec((None, 1, nh), lambda b, pt: (b, 0, 0))],
        out_specs=pl.BlockSpec((None, n_pages * ps, nh), lambda b, pt: (b, 0, 0)),
    )
    return pl.pallas_call(
        functools.partial(_suffix_kernel, n_pages=n_pages),
        grid_spec=grid_spec,
        out_shape=jax.ShapeDtypeStruct((n, n_pages * ps, nh), F32),
        compiler_params=_params(("arbitrary",)),
        name="forget_suffix",
    )(page_table.reshape(-1), *([cache_logf] * n_pages), lf_new)


def _decode_kernel(pt_ref, *refs, pages):
    del pt_ref
    k_pages = refs[:pages]
    v_pages = refs[pages:2 * pages]
    q_ref, kn_ref, vn_ref, d_ref, o_ref, m_sc, l_sc, acc_sc, kbuf, vbuf = refs[2 * pages:]
    step = pl.program_id(1)
    nh, d_b = acc_sc.shape
    ps = k_pages[0].shape[0]
    head_of_lane = lax.broadcasted_iota(jnp.int32, (nh, d_b), 1) // FOX_HD
    own = head_of_lane == lax.broadcasted_iota(jnp.int32, (nh, d_b), 0)
    q_rows = jnp.where(own, q_ref[...].astype(F32), 0.0).astype(BF16)

    @pl.when(step == 0)
    def _():
        k_new = kn_ref[...].astype(BF16).astype(F32)
        m_sc[...] = jnp.sum(q_rows.astype(F32) * k_new, axis=-1, keepdims=True)
        l_sc[...] = jnp.ones_like(l_sc)
        acc_sc[...] = jnp.broadcast_to(vn_ref[...], (nh, d_b))

    for j in range(pages):
        kbuf[j * ps:(j + 1) * ps, :] = k_pages[j][...]
        vbuf[j * ps:(j + 1) * ps, :] = v_pages[j][...]
    s = _dot_nt(q_rows, kbuf[...]) + d_ref[...]
    m_old = m_sc[...]
    m_new = jnp.maximum(m_old, jnp.max(s, axis=-1, keepdims=True))
    alpha = jnp.exp2(m_old - m_new)
    p = jnp.exp2(s - m_new)
    l_sc[...] = alpha * l_sc[...] + jnp.sum(p, axis=-1, keepdims=True)
    acc_sc[...] = alpha * acc_sc[...] + _dot(p.astype(BF16), vbuf[...])
    m_sc[...] = m_new

    @pl.when(step == pl.num_programs(1) - 1)
    def _():
        o_ref[...] = jnp.sum(jnp.where(own, acc_sc[...] / l_sc[...], 0.0), axis=0, keepdims=True)


def _decode_attn(q, k_new, v_new, bias_t, cache_k, cache_v, page_table):
    n, n_pages = page_table.shape
    _, ps, d_b = cache_k.shape
    nh = bias_t.shape[1]
    pages = min(DEC_PAGES, n_pages)
    steps = n_pages // pages

    def page_spec(j):
        return pl.BlockSpec((None, ps, d_b), lambda b, s, pt: (pt[b * n_pages + s * pages + j], 0, 0))

    row_spec = pl.BlockSpec((None, 1, d_b), lambda b, s, pt: (b, 0, 0))
    grid_spec = pltpu.PrefetchScalarGridSpec(
        num_scalar_prefetch=1,
        grid=(n, steps),
        in_specs=[page_spec(j) for j in range(pages)] + [page_spec(j) for j in range(pages)]
        + [row_spec, row_spec, row_spec,
           pl.BlockSpec((None, nh, pages * ps), lambda b, s, pt: (b, 0, s))],
        out_specs=row_spec,
        scratch_shapes=[pltpu.VMEM((nh, 1), F32), pltpu.VMEM((nh, 1), F32), pltpu.VMEM((nh, d_b), F32),
                        pltpu.VMEM((pages * ps, d_b), BF16), pltpu.VMEM((pages * ps, d_b), BF16)],
    )
    return pl.pallas_call(
        functools.partial(_decode_kernel, pages=pages),
        grid_spec=grid_spec,
        out_shape=jax.ShapeDtypeStruct((n, 1, d_b), F32),
        compiler_params=_params(("parallel", "arbitrary")),
        name="fox_decode_attn",
    )(page_table.reshape(-1), *([cache_k] * pages), *([cache_v] * pages), q, k_new, v_new, bias_t)


def kernel(x_prompt, x_sample, c_prompt, c_sample, state_gla, cache_k, cache_v, cache_logf, page_table,
           ada_w, ada_b, gla_w_in, gla_w_g2, gla_b_g, gla_onorm, gla_w_out, kv_ada_w, kv_ada_b, w_kv, b_f,
           k_norm, fox_w_in, q_norm, fox_w_out):
    bp, seq, d = x_prompt.shape
    ns = x_sample.shape[0]
    n_a = gla_w_in.shape[0]
    depth = ada_w.shape[0]
    nh = b_f.shape[0]
    d_b = nh * FOX_HD
    dk, dv = state_gla.shape[-2:]

    pad = (-(ns + bp)) % 8
    c_all = jnp.concatenate([c_sample, c_prompt, jnp.zeros((pad, d), F32)], axis=0)
    mod = _ada(c_all, ada_w, ada_b)
    kv_mod = _ada(c_all, kv_ada_w[None], kv_ada_b[None])[0]

    def split_mod(m, n_parts):
        parts = jnp.split(m, n_parts, axis=-1)
        return ([p[ns:ns + bp, None, :] for p in parts],
                [p[None, :ns, :] for p in parts])

    xp = x_prompt
    xs = x_sample.reshape(1, ns, d)
    sp_states, ss_states = [], []
    for layer in range(n_a):
        mod_p, mod_s = split_mod(mod[layer], 3)
        q, k, v, g, la = _gla_in(xp, mod_p[0], mod_p[1], gla_w_in[layer], gla_w_g2[layer], gla_b_g[layer])
        o, s_fin = _gla_scan(q, k, v, la)
        sp_states.append(s_fin)
        xp = _out_proj(o, g, xp, mod_p[2], gla_w_out[layer], gla_onorm[layer])
        q, k, v, g, la = _gla_in(xs, mod_s[0], mod_s[1], gla_w_in[layer], gla_w_g2[layer], gla_b_g[layer])
        o, s_new = _gla_step(q[0], k[0], la[0], v[0], state_gla[layer])
        ss_states.append(s_new)
        xs = _out_proj(o.reshape(1, ns, -1), g, xs, mod_s[2], gla_w_out[layer], gla_onorm[layer])

    kvm_p, kvm_s = split_mod(kv_mod, 2)
    k_p, v_p, lf_p, kb_p, vb_p = _kv_proj(xp, kvm_p[0], kvm_p[1], w_kv, b_f, k_norm)
    k_s, v_s, lf_s, _, _ = _kv_proj(xs, kvm_s[0], kvm_s[1], w_kv, b_f, k_norm)
    vt_p, fq_p, fk_p = _fox_attn_layouts(vb_p, _cumsum_rows(lf_p), min(ATTN_TK, seq))
    lf_new = lf_s.reshape(ns, 1, nh)
    bias_t = jnp.swapaxes(_forget_suffix(cache_logf, page_table, lf_new), 1, 2)
    ck = cache_k.reshape(cache_k.shape[0], cache_k.shape[1], d_b).astype(BF16)
    cv = cache_v.reshape(cache_v.shape[0], cache_v.shape[1], d_b).astype(BF16)
    k_new = k_s.reshape(ns, 1, d_b)
    v_new = v_s.reshape(ns, 1, d_b)

    for j in range(depth - n_a):
        mod_p, mod_s = split_mod(mod[n_a + j], 3)
        q, g = _fox_in(xp, mod_p[0], mod_p[1], fox_w_in[j], q_norm[j])
        o = _fox_attn(q, kb_p, vt_p, fq_p, fk_p)
        xp = _out_proj(o, g, xp, mod_p[2], fox_w_out[j])
        q, g = _fox_in(xs, mod_s[0], mod_s[1], fox_w_in[j], q_norm[j])
        o = _decode_attn(q.reshape(ns, 1, d_b), k_new, v_new, bias_t, ck, cv, page_table)
        xs = _out_proj(o.reshape(1, ns, d_b), g, xs, mod_s[2], fox_w_out[j])

    return (xp, xs.reshape(ns, 1, d),
            jnp.stack(sp_states), jnp.stack(ss_states),
            k_p.reshape(bp, seq, nh, FOX_HD), v_p.reshape(bp, seq, nh, FOX_HD), lf_p,
            k_new.reshape(ns, 1, nh, FOX_HD), v_new.reshape(ns, 1, nh, FOX_HD), lf_new)
```

```python
import functools

import jax
import jax.numpy as jnp
from jax import lax
from jax.experimental import pallas as pl
from jax.experimental.pallas import tpu as pltpu

F32 = jnp.float32
BF16 = jnp.bfloat16

EPS = 1e-6
GATE_TAU = 16.0
GLA_HEADS = 4
GLA_CHUNK = 64
GLA_SUB = 16
FOX_HD = 64
LANES = 128
NEG_BIG = -1e30
LOG2E = 1.4426950408889634
VMEM_LIMIT = 56 * 1024 * 1024

ROW_TILE = 512
ATTN_TK = 512
SCAN_T = 512
GATHER_PAGES = 4
GLA_STEP_G = 4


def _params(sem):
    return pltpu.CompilerParams(dimension_semantics=sem, vmem_limit_bytes=VMEM_LIMIT)


def _silu(x):
    return x / (1.0 + jnp.exp(-x))


def _log_sigmoid(x):
    return jnp.minimum(x, 0.0) - jnp.log1p(jnp.exp(-jnp.abs(x)))


def _rms(x):
    return x * lax.rsqrt(jnp.mean(x * x, axis=-1, keepdims=True) + EPS)


def _dot(a, b):
    return jnp.dot(a, b, preferred_element_type=F32)


def _dot_nt(a, b):
    return lax.dot_general(a, b, (((1,), (1,)), ((), ())), preferred_element_type=F32)


def _dot_tn(a, b):
    return lax.dot_general(a, b, (((0,), (0,)), ((), ())), preferred_element_type=F32)


def _split3(x):
    p0 = x.astype(BF16)
    r1 = x - p0.astype(F32)
    p1 = r1.astype(BF16)
    p2 = (r1 - p1.astype(F32)).astype(BF16)
    return p0, p1, p2


def _exact_left_mul(mat01, x):
    p0, p1, p2 = _split3(x)
    return _dot(mat01, p0) + _dot(mat01, p1) + _dot(mat01, p2)


def _modulated_norm(x_ref, shift_ref, scale_ref):
    h = _rms(x_ref[...]) * (1.0 + scale_ref[...]) + shift_ref[...]
    return h.astype(BF16)


def _head_rms64(p):
    lane = lax.broadcasted_iota(jnp.int32, (1, LANES), 1)
    low = lane < FOX_HD
    outs = []
    for c in range(p.shape[-1] // LANES):
        blk = p[:, c * LANES:(c + 1) * LANES]
        sq = blk * blk
        s_lo = jnp.sum(jnp.where(low, sq, 0.0), axis=-1, keepdims=True)
        s_hi = jnp.sum(jnp.where(low, 0.0, sq), axis=-1, keepdims=True)
        r_lo = lax.rsqrt(s_lo * (1.0 / FOX_HD) + EPS)
        r_hi = lax.rsqrt(s_hi * (1.0 / FOX_HD) + EPS)
        outs.append(blk * jnp.where(low, r_lo, r_hi))
    return jnp.concatenate(outs, axis=-1)


def _ada_kernel(c_ref, w_ref, b_ref, o_ref):
    a = _silu(c_ref[...]).astype(BF16)
    o_ref[...] = _dot(a, w_ref[...].astype(BF16)) + b_ref[...]


def _ada(c_all, w, b, tn=1024):
    nl, d, n = w.shape
    rows = c_all.shape[0]
    return pl.pallas_call(
        _ada_kernel,
        grid=(nl, n // tn),
        in_specs=[
            pl.BlockSpec((rows, d), lambda l, j: (0, 0)),
            pl.BlockSpec((None, d, tn), lambda l, j: (l, 0, j)),
            pl.BlockSpec((None, 1, tn), lambda l, j: (l, 0, j)),
        ],
        out_specs=pl.BlockSpec((None, rows, tn), lambda l, j: (l, 0, j)),
        out_shape=jax.ShapeDtypeStruct((nl, rows, n), F32),
        compiler_params=_params(("parallel", "parallel")),
        name="ada_mod",
    )(c_all, w, b.reshape(nl, 1, n))


def _row_spec(tm, width):
    return pl.BlockSpec((None, tm, width), lambda b, i: (b, i, 0))


def _mod_spec(mod, tm):
    if mod.shape[1] == 1:
        return pl.BlockSpec((None, 1, mod.shape[2]), lambda b, i: (b, 0, 0))
    return pl.BlockSpec((None, tm, mod.shape[2]), lambda b, i: (b, i, 0))


def _full_spec(a):
    nd = a.ndim
    return pl.BlockSpec(a.shape, lambda b, i: (0,) * nd)


def _row_tile(length):
    return min(ROW_TILE, length)


def _gla_in_kernel(x_ref, sh_ref, sc_ref, wq_ref, wk_ref, wv_ref, wg_ref, wz_ref, wg2_ref, bg_ref,
                   q_ref, k_ref, v_ref, g_ref, la_ref, *, q_scale):
    h = _modulated_norm(x_ref, sh_ref, sc_ref)
    q_ref[...] = _dot(h, wq_ref[...]) * q_scale
    k_ref[...] = _dot(h, wk_ref[...])
    v_ref[...] = _dot(h, wv_ref[...])
    g_ref[...] = _dot(h, wg_ref[...])
    z = _dot(h, wz_ref[...])
    t = _dot(z.astype(BF16), wg2_ref[...]) + bg_ref[...]
    la_ref[...] = _log_sigmoid(t) * (1.0 / GATE_TAU)


def _gla_in(x, shift, scale, w_in, w_g2, b_g):
    bsz, length, d = x.shape
    dk_tot = w_g2.shape[1]
    rank = w_g2.shape[0]
    dv_tot = (w_in.shape[1] - 2 * dk_tot - rank) // 2
    dk = dk_tot // GLA_HEADS
    wb = w_in.astype(BF16)
    wq = wb[:, :dk_tot]
    wk = wb[:, dk_tot:2 * dk_tot]
    wv = wb[:, 2 * dk_tot:2 * dk_tot + dv_tot]
    wg = wb[:, 2 * dk_tot + dv_tot:2 * dk_tot + 2 * dv_tot]
    wz = wb[:, 2 * dk_tot + 2 * dv_tot:]
    wg2 = w_g2.astype(BF16)
    bg = b_g.reshape(1, dk_tot)
    tm = _row_tile(length)
    weights = (wq, wk, wv, wg, wz, wg2, bg)
    widths = (dk_tot, dk_tot, dv_tot, dv_tot, dk_tot)
    return pl.pallas_call(
        functools.partial(_gla_in_kernel, q_scale=dk ** -0.5),
        grid=(bsz, length // tm),
        in_specs=[_row_spec(tm, d), _mod_spec(shift, tm), _mod_spec(scale, tm)]
        + [_full_spec(w) for w in weights],
        out_specs=[_row_spec(tm, w) for w in widths],
        out_shape=[jax.ShapeDtypeStruct((bsz, length, w), F32) for w in widths],
        compiler_params=_params(("parallel", "parallel")),
        name="gla_in_proj",
    )(x, shift, scale, *weights)


def _gla_chunk_head(q, k, v, b, s_old, rows):
    c, dk = q.shape
    nsub = c // GLA_SUB
    b_last = b[c - 1:c, :]
    o = _dot((q * jnp.exp(b)).astype(BF16), s_old.astype(BF16))
    kd = k * jnp.exp(b_last - b)
    upd = _dot_tn(kd.astype(BF16), v.astype(BF16))
    decay_col = jnp.exp(jnp.transpose(jnp.broadcast_to(b_last, (dk, dk))))
    s_new = jnp.concatenate([decay_col] * (s_old.shape[1] // dk), axis=1) * s_old + upd
    qcat, kcat = [], []
    for j in range(nsub - 1):
        r_j = b[(j + 1) * GLA_SUB - 1:(j + 1) * GLA_SUB, :]
        in_j = (rows >= j * GLA_SUB) & (rows < (j + 1) * GLA_SUB)
        after_j = rows >= (j + 1) * GLA_SUB
        kcat.append(jnp.where(in_j, k * jnp.exp(jnp.where(in_j, r_j - b, 0.0)), 0.0))
        qcat.append(jnp.where(after_j, q * jnp.exp(jnp.where(after_j, b - r_j, 0.0)), 0.0))
    att = _dot_nt(jnp.concatenate(qcat, axis=1).astype(BF16), jnp.concatenate(kcat, axis=1).astype(BF16))
    sub_row = lax.broadcasted_iota(jnp.int32, (GLA_SUB, 1), 0)
    col_id = lax.broadcasted_iota(jnp.int32, (1, c), 1)
    diag_blocks = []
    for j in range(nsub):
        lo = j * GLA_SUB
        qj = q[lo:lo + GLA_SUB, :]
        bj = b[lo:lo + GLA_SUB, :]
        blk = jnp.zeros((GLA_SUB, c), F32)
        for t in range(GLA_SUB):
            src = lo + t
            valid = sub_row >= t
            e = jnp.exp(jnp.where(valid, bj - b[src:src + 1, :], 0.0))
            col = jnp.sum(qj * k[src:src + 1, :] * e, axis=-1, keepdims=True)
            blk = jnp.where((col_id == src) & valid, col, blk)
        diag_blocks.append(blk)
    att = att + jnp.concatenate(diag_blocks, axis=0)
    o = o + _dot(att.astype(BF16), v.astype(BF16))
    return o, s_new


def _gla_scan_kernel(q_ref, k_ref, v_ref, la_ref, o_ref, sfin_ref, s_sc, *, n_chunks):
    t = pl.program_id(1)
    dk = q_ref.shape[-1] // GLA_HEADS
    dv = v_ref.shape[-1] // GLA_HEADS
    c = GLA_CHUNK

    @pl.when(t == 0)
    def _():
        s_sc[...] = jnp.zeros_like(s_sc)

    tri = (lax.broadcasted_iota(jnp.int32, (c, c), 0) >= lax.broadcasted_iota(jnp.int32, (c, c), 1))
    tri = jnp.where(tri, 1.0, 0.0).astype(BF16)
    rows = lax.broadcasted_iota(jnp.int32, (c, 1), 0)

    def chunk(ci, carry):
        r0 = pl.multiple_of(ci * c, c)
        b_all = _exact_left_mul(tri, la_ref[pl.ds(r0, c), :])
        for h in range(GLA_HEADS):
            o, s_new = _gla_chunk_head(
                q_ref[pl.ds(r0, c), h * dk:(h + 1) * dk],
                k_ref[pl.ds(r0, c), h * dk:(h + 1) * dk],
                v_ref[pl.ds(r0, c), h * dv:(h + 1) * dv],
                b_all[:, h * dk:(h + 1) * dk],
                s_sc[h], rows)
            o_ref[pl.ds(r0, c), h * dv:(h + 1) * dv] = o
            s_sc[h] = s_new
        return carry

    lax.fori_loop(0, n_chunks, chunk, 0)

    @pl.when(t == pl.num_programs(1) - 1)
    def _():
        sfin_ref[...] = s_sc[...]


def _gla_scan(q, k, v, la):
    bsz, length, dk_tot = q.shape
    dv_tot = v.shape[-1]
    dk, dv = dk_tot // GLA_HEADS, dv_tot // GLA_HEADS
    tt = min(SCAN_T, length)
    return pl.pallas_call(
        functools.partial(_gla_scan_kernel, n_chunks=tt // GLA_CHUNK),
        grid=(bsz, length // tt),
        in_specs=[_row_spec(tt, dk_tot), _row_spec(tt, dk_tot), _row_spec(tt, dv_tot), _row_spec(tt, dk_tot)],
        out_specs=[_row_spec(tt, dv_tot),
                   pl.BlockSpec((None, GLA_HEADS, dk, dv), lambda b, i: (b, 0, 0, 0))],
        out_shape=[jax.ShapeDtypeStruct((bsz, length, dv_tot), F32),
                   jax.ShapeDtypeStruct((bsz, GLA_HEADS, dk, dv), F32)],
        scratch_shapes=[pltpu.VMEM((GLA_HEADS, dk, dv), F32)],
        compiler_params=_params(("parallel", "arbitrary")),
        name="gla_scan",
    )(q, k, v, la)


def _gla_step_kernel(qt_ref, kt_ref, lat_ref, v_ref, s_ref, *rest, n_prev):
    if n_prev:
        prev_ref, o_ref, so_ref = rest
        so_ref[0:n_prev] = prev_ref[...]
    else:
        o_ref, so_ref = rest
    dk, dv = s_ref.shape[-2:]
    for g in range(s_ref.shape[0]):
        for h in range(GLA_HEADS):
            q_col = qt_ref[h * dk:(h + 1) * dk, g:g + 1]
            k_col = kt_ref[h * dk:(h + 1) * dk, g:g + 1]
            a_col = jnp.exp(lat_ref[h * dk:(h + 1) * dk, g:g + 1])
            v_row = v_ref[g:g + 1, h * dv:(h + 1) * dv]
            s_new = a_col * s_ref[g, h] + k_col * v_row
            so_ref[n_prev, g, h] = s_new
            o_ref[g:g + 1, h * dv:(h + 1) * dv] = jnp.sum(q_col * s_new, axis=0, keepdims=True)


def _gla_step(q, k, la, v, state, prev=None):
    n, dk_tot = q.shape
    dv_tot = v.shape[-1]
    dk, dv = state.shape[-2:]
    n_prev = 0 if prev is None else prev.shape[0]
    g = GLA_STEP_G
    steps = n // g

    def cols(a):
        return a.reshape(steps, g, a.shape[-1]).transpose(0, 2, 1)

    col_spec = pl.BlockSpec((None, dk_tot, g), lambda i: (i, 0, 0))
    prev_in = [] if prev is None else [prev]
    prev_spec = [] if prev is None else [pl.BlockSpec((n_prev, g, GLA_HEADS, dk, dv), lambda i: (0, i, 0, 0, 0))]
    o, stacked = pl.pallas_call(
        functools.partial(_gla_step_kernel, n_prev=n_prev),
        grid=(steps,),
        in_specs=[col_spec, col_spec, col_spec,
                  pl.BlockSpec((None, g, dv_tot), lambda i: (i, 0, 0)),
                  pl.BlockSpec((g, GLA_HEADS, dk, dv), lambda i: (i, 0, 0, 0))] + prev_spec,
        out_specs=[pl.BlockSpec((None, g, dv_tot), lambda i: (i, 0, 0)),
                   pl.BlockSpec((n_prev + 1, g, GLA_HEADS, dk, dv), lambda i: (0, i, 0, 0, 0))],
        out_shape=[jax.ShapeDtypeStruct((steps, g, dv_tot), F32),
                   jax.ShapeDtypeStruct((n_prev + 1,) + state.shape, F32)],
        compiler_params=_params(("parallel",)),
        name="gla_step",
    )(cols(q), cols(k), cols(la), v.reshape(steps, g, dv_tot), state, *prev_in)
    return o.reshape(n, dv_tot), stacked


def _out_kernel(o_ref, g_ref, x_ref, gate_ref, w_ref, *rest, head_dim):
    if head_dim:
        onorm_ref, y_ref = rest
        o = o_ref[...]
        o = jnp.concatenate(
            [_rms(o[:, h * head_dim:(h + 1) * head_dim]) for h in range(o.shape[-1] // head_dim)], axis=-1)
        o = o * onorm_ref[...]
    else:
        (y_ref,) = rest
        o = o_ref[...]
    a = (o * _silu(g_ref[...])).astype(BF16)
    y_ref[...] = x_ref[...] + gate_ref[...] * _dot(a, w_ref[...])


def _out_proj(o, g, x, gate, w_out, onorm=None):
    bsz, length, d = x.shape
    width = o.shape[-1]
    tm = _row_tile(length)
    wb = w_out.astype(BF16)
    extra, head_dim = [], 0
    if onorm is not None:
        head_dim = onorm.shape[0]
        extra = [jnp.tile(onorm, width // head_dim).reshape(1, width)]
    return pl.pallas_call(
        functools.partial(_out_kernel, head_dim=head_dim),
        grid=(bsz, length // tm),
        in_specs=[_row_spec(tm, width), _row_spec(tm, width), _row_spec(tm, d), _mod_spec(gate, tm),
                  _full_spec(wb)] + [_full_spec(e) for e in extra],
        out_specs=_row_spec(tm, d),
        out_shape=jax.ShapeDtypeStruct((bsz, length, d), F32),
        compiler_params=_params(("parallel", "parallel")),
        name="out_proj",
    )(o, g, x, gate, wb, *extra)


def _kv_kernel(x_ref, sh_ref, sc_ref, wk_ref, wv_ref, wf_ref, bf_ref, kn_ref,
               k_ref, v_ref, lf_ref, kb_ref, vb_ref):
    h = _modulated_norm(x_ref, sh_ref, sc_ref)
    k = _head_rms64(_dot(h, wk_ref[...])) * kn_ref[...]
    v = _dot(h, wv_ref[...])
    nh = lf_ref.shape[-1]
    tm = x_ref.shape[0]
    for hd in range(nh):
        k_ref[pl.ds(hd, tm, stride=nh), :] = k[:, hd * FOX_HD:(hd + 1) * FOX_HD]
        v_ref[pl.ds(hd, tm, stride=nh), :] = v[:, hd * FOX_HD:(hd + 1) * FOX_HD]
    kb_ref[...] = k.astype(BF16)
    vb_ref[...] = v.astype(BF16)
    lf_ref[...] = _log_sigmoid(_dot(h, wf_ref[...]) + bf_ref[...])


def _kv_proj(x, shift, scale, w_kv, b_f, k_norm):
    bsz, length, d = x.shape
    nh = b_f.shape[0]
    d_b = (w_kv.shape[1] - nh) // 2
    wb = w_kv.astype(BF16)
    weights = (wb[:, :d_b], wb[:, d_b:2 * d_b], wb[:, 2 * d_b:], b_f.reshape(1, nh),
               jnp.tile(k_norm, nh).reshape(1, d_b))
    tm = _row_tile(length)
    hd = d_b // nh
    outs = ((nh, hd, F32), (nh, hd, F32), (1, nh, F32), (1, d_b, BF16), (1, d_b, BF16))
    k4, v4, lf, kb, vb = pl.pallas_call(
        _kv_kernel,
        grid=(bsz, length // tm),
        in_specs=[_row_spec(tm, d), _mod_spec(shift, tm), _mod_spec(scale, tm)]
        + [_full_spec(w) for w in weights],
        out_specs=[_row_spec(tm * r, w) for r, w, _ in outs],
        out_shape=[jax.ShapeDtypeStruct((bsz, length * r, w), dt) for r, w, dt in outs],
        compiler_params=_params(("parallel", "parallel")),
        name="kv_proj",
    )(x, shift, scale, *weights)
    return k4.reshape(bsz, length, nh, hd), v4.reshape(bsz, length, nh, hd), lf, kb, vb


def _fox_in_kernel(x_ref, sh_ref, sc_ref, wq_ref, wg_ref, qn_ref, q_ref, g_ref):
    h = _modulated_norm(x_ref, sh_ref, sc_ref)
    q_ref[...] = (_head_rms64(_dot(h, wq_ref[...])) * qn_ref[...]).astype(BF16)
    g_ref[...] = _dot(h, wg_ref[...])


def _fox_in(x, shift, scale, w_in, q_norm):
    bsz, length, d = x.shape
    d_b = w_in.shape[1] // 2
    wb = w_in.astype(BF16)
    qn = (jnp.tile(q_norm, d_b // FOX_HD) * (FOX_HD ** -0.5 * LOG2E)).reshape(1, d_b)
    weights = (wb[:, :d_b], wb[:, d_b:], qn)
    tm = _row_tile(length)
    return pl.pallas_call(
        _fox_in_kernel,
        grid=(bsz, length // tm),
        in_specs=[_row_spec(tm, d), _mod_spec(shift, tm), _mod_spec(scale, tm)]
        + [_full_spec(w) for w in weights],
        out_specs=[_row_spec(tm, d_b), _row_spec(tm, d_b)],
        out_shape=[jax.ShapeDtypeStruct((bsz, length, d_b), BF16),
                   jax.ShapeDtypeStruct((bsz, length, d_b), F32)],
        compiler_params=_params(("parallel", "parallel")),
        name="fox_in_proj",
    )(x, shift, scale, *weights)


def _cumsum_kernel(lf_ref, f_ref, carry_sc):
    r = lf_ref.shape[0]

    @pl.when(pl.program_id(1) == 0)
    def _():
        carry_sc[...] = jnp.zeros_like(carry_sc)

    tri = lax.broadcasted_iota(jnp.int32, (r, r), 0) >= lax.broadcasted_iota(jnp.int32, (r, r), 1)
    tri = jnp.where(tri, 1.0, 0.0).astype(BF16)
    f = _exact_left_mul(tri, lf_ref[...]) + carry_sc[...]
    f_ref[...] = f * LOG2E
    carry_sc[...] = f[r - 1:r, :]


def _cumsum_rows(lf):
    bsz, length, nh = lf.shape
    r = min(ROW_TILE, length)
    return pl.pallas_call(
        _cumsum_kernel,
        grid=(bsz, length // r),
        in_specs=[_row_spec(r, nh)],
        out_specs=_row_spec(r, nh),
        out_shape=jax.ShapeDtypeStruct(lf.shape, F32),
        scratch_shapes=[pltpu.VMEM((1, nh), F32)],
        compiler_params=_params(("parallel", "arbitrary")),
        name="forget_cumsum",
    )(lf)


N_PIECES = 3


def _key_aug_kernel(kb_ref, f_ref, place_k_ref, place_f_ref, ones_ref, ka_ref):
    out = _dot(kb_ref[...], place_k_ref[...]) + ones_ref[...]
    for j, piece in enumerate(_split3(f_ref[...])):
        out = out - _dot(piece, place_f_ref[j])
    for h in range(ka_ref.shape[0]):
        ka_ref[h] = out[:, h * LANES:(h + 1) * LANES].astype(BF16)


def _key_aug(kb, f2):
    bsz, length, d_b = kb.shape
    nh = f2.shape[-1]
    tm = _row_tile(length)
    col = jnp.arange(nh * LANES)
    row = jnp.arange(d_b)
    place_k = ((col[None, :] // LANES == row[:, None] // FOX_HD)
               & (col[None, :] % LANES == row[:, None] % FOX_HD)).astype(BF16)
    head = jnp.arange(nh)
    place_f = jnp.stack([(col[None, :] == head[:, None] * LANES + FOX_HD + j) for j in range(N_PIECES)]
                        ).astype(BF16)
    ones = ((col % LANES >= FOX_HD + N_PIECES) & (col % LANES < FOX_HD + 2 * N_PIECES)).astype(F32)[None, :]
    return pl.pallas_call(
        _key_aug_kernel,
        grid=(bsz, length // tm),
        in_specs=[_row_spec(tm, d_b), _row_spec(tm, nh), _full_spec(place_k), _full_spec(place_f),
                  _full_spec(ones)],
        out_specs=pl.BlockSpec((None, nh, tm, LANES), lambda b, i: (b, 0, i, 0)),
        out_shape=jax.ShapeDtypeStruct((bsz, nh, length, LANES), BF16),
        compiler_params=_params(("parallel", "parallel")),
        name="fox_key_aug",
    )(kb, f2, place_k, place_f, ones)


def _fox_attn_kernel(q_ref, ka_ref, vt_ref, fq_ref, o_ref, qt_sc, m_sc, l_sc, acc_sc):
    qi = pl.program_id(2)
    tq = q_ref.shape[0]
    tk = tq
    q_t = jnp.transpose(q_ref[...].astype(F32))
    brow = lax.broadcasted_iota(jnp.int32, (FOX_HD, 1), 0)
    ones_rows = jnp.broadcast_to(jnp.where(brow < N_PIECES, 1.0, 0.0), (FOX_HD, tq))

    def q_aug(i, c):
        extra = ones_rows
        if c is not None:
            for j, piece in enumerate(_split3(c)):
                extra = jnp.where(brow == N_PIECES + j, piece.astype(F32), extra)
        return jnp.concatenate([q_t[i * FOX_HD:(i + 1) * FOX_HD, :], extra], axis=0).astype(BF16)

    q_plain = (q_aug(0, None), q_aug(1, None))

    def reset():
        m_sc[...] = jnp.full_like(m_sc, NEG_BIG)
        l_sc[...] = jnp.zeros_like(l_sc)
        acc_sc[...] = jnp.zeros_like(acc_sc)

    def online_block(kj, causal):
        c0 = pl.multiple_of(kj * tk, tk)
        ts = []
        for i in range(2):
            t = _dot(ka_ref[i, pl.ds(c0, tk), :], q_plain[i])
            if causal:
                key = lax.broadcasted_iota(jnp.int32, (tk, tq), 0)
                qry = lax.broadcasted_iota(jnp.int32, (tk, tq), 1)
                t = jnp.where(key <= qry, t, NEG_BIG)
            ts.append(t)
        ps, alphas = [], []
        for i in range(2):
            fq = fq_ref[i:i + 1, :]
            m_old = m_sc[i]
            m_new = jnp.maximum(m_old, jnp.max(ts[i], axis=0, keepdims=True) + fq)
            alpha = jnp.exp2(m_old - m_new)
            p = jnp.exp2(ts[i] + (fq - m_new))
            l_sc[i] = alpha * l_sc[i] + jnp.sum(p, axis=0, keepdims=True)
            m_sc[i] = m_new
            ps.append(p.astype(BF16))
            alphas.append(alpha)
        for i in range(2):
            acc_sc[i] = alphas[i] * acc_sc[i] + _dot(vt_ref[kj, i * FOX_HD:(i + 1) * FOX_HD, :], ps[i])

    def result():
        return jnp.concatenate([acc_sc[0] / l_sc[0], acc_sc[1] / l_sc[1]], axis=0)

    reset()
    online_block(qi, True)
    for i in range(2):
        qt_sc[i] = q_aug(i, fq_ref[i:i + 1, :] - m_sc[i])

    def fixed_shift_blocks(kjs):
        ps, sums = [], [0.0, 0.0]
        for kj in kjs:
            c0 = pl.multiple_of(kj * tk, tk)
            for i in range(2):
                p = jnp.exp2(_dot(ka_ref[i, pl.ds(c0, tk), :], qt_sc[i]))
                sums[i] = sums[i] + jnp.sum(p, axis=0, keepdims=True)
                ps.append(p.astype(BF16))
        for i in range(2):
            l_sc[i] = l_sc[i] + sums[i]
            pv = 0.0
            for n, kj in enumerate(kjs):
                pv = pv + _dot(vt_ref[kj, i * FOX_HD:(i + 1) * FOX_HD, :], ps[2 * n + i])
            acc_sc[i] = acc_sc[i] + pv

    def quad_body(j, carry):
        fixed_shift_blocks(tuple(4 * j + n for n in range(4)))
        return carry

    lax.fori_loop(0, qi // 4, quad_body, 0)
    done = (qi // 4) * 4

    @pl.when(qi % 4 >= 2)
    def _():
        fixed_shift_blocks((done, done + 1))

    @pl.when(qi % 2 == 1)
    def _():
        fixed_shift_blocks((qi - 1,))
    o_t = result()
    bad = (jnp.sum(jnp.where(jnp.isfinite(o_t), 0.0, 1.0))
           + jnp.sum(jnp.where(jnp.isfinite(l_sc[...]), 0.0, 1.0)))

    @pl.when(bad == 0.0)
    def _():
        o_ref[...] = jnp.transpose(o_t)

    @pl.when(bad != 0.0)
    def _():
        reset()

        def body(kj, carry):
            online_block(kj, False)
            return carry

        lax.fori_loop(0, qi, body, 0)
        online_block(qi, True)
        o_ref[...] = jnp.transpose(result())


def _fox_attn_layouts(vb, f2, tk):
    bsz, length, d_b = vb.shape
    pairs = f2.shape[-1] // 2
    fq = f2.reshape(bsz, length, pairs, 2).transpose(0, 2, 3, 1)
    vt = vb.reshape(bsz, length // tk, tk, pairs, LANES).transpose(0, 3, 1, 4, 2)
    return vt, fq


def _fox_attn(q, ka, vt, fq):
    bsz, length, d_b = q.shape
    pairs = fq.shape[1]
    nk, _, tk = vt.shape[2:]
    tq = tk
    return pl.pallas_call(
        _fox_attn_kernel,
        grid=(bsz, pairs, length // tq),
        in_specs=[
            pl.BlockSpec((None, tq, LANES), lambda b, p, i: (b, i, p)),
            pl.BlockSpec((None, 2, length, LANES), lambda b, p, i: (b, p, 0, 0)),
            pl.BlockSpec((None, None, nk, LANES, tk), lambda b, p, i: (b, p, 0, 0, 0)),
            pl.BlockSpec((None, None, 2, tq), lambda b, p, i: (b, p, 0, i)),
        ],
        out_specs=pl.BlockSpec((None, tq, LANES), lambda b, p, i: (b, i, p)),
        out_shape=jax.ShapeDtypeStruct((bsz, length, d_b), F32),
        scratch_shapes=[pltpu.VMEM((2, LANES, tq), BF16),
                        pltpu.VMEM((2, 1, tq), F32), pltpu.VMEM((2, 1, tq), F32),
                        pltpu.VMEM((2, FOX_HD, tq), F32)],
        compiler_params=_params(("parallel", "parallel", "arbitrary")),
        name="fox_attn",
    )(q, ka, vt, fq)


def _suffix_kernel(pt_ref, *refs, n_pages):
    del pt_ref
    page_refs = refs[:n_pages]
    new_ref, d_ref = refs[n_pages:]
    ps = page_refs[0].shape[0]
    upper = lax.broadcasted_iota(jnp.int32, (ps, ps), 1) > lax.broadcasted_iota(jnp.int32, (ps, ps), 0)
    upper = jnp.where(upper, 1.0, 0.0).astype(BF16)
    carry = new_ref[...]
    for j in reversed(range(n_pages)):
        page = page_refs[j][...]
        d_ref[j * ps:(j + 1) * ps, :] = (_exact_left_mul(upper, page) + carry) * LOG2E
        carry = carry + jnp.sum(page, axis=0, keepdims=True)


def _forget_suffix(cache_logf, page_table, lf_new):
    n, n_pages = page_table.shape
    _, ps, nh = cache_logf.shape

    def page_spec(j):
        return pl.BlockSpec((None, ps, nh), lambda b, pt: (pt[b * n_pages + j], 0, 0))

    grid_spec = pltpu.PrefetchScalarGridSpec(
        num_scalar_prefetch=1,
        grid=(n,),
        in_specs=[page_spec(j) for j in range(n_pages)]
        + [pl.BlockSpec((None, 1, nh), lambda b, pt: (b, 0, 0))],
        out_specs=pl.BlockSpec((None, n_pages * ps, nh), lambda b, pt: (b, 0, 0)),
    )
    return pl.pallas_call(
        functools.partial(_suffix_kernel, n_pages=n_pages),
        grid_spec=grid_spec,
        out_shape=jax.ShapeDtypeStruct((n, n_pages * ps, nh), F32),
        compiler_params=_params(("arbitrary",)),
        name="forget_suffix",
    )(page_table.reshape(-1), *([cache_logf] * n_pages), lf_new)


def _gather_kernel(pt_ref, *refs, pages):
    del pt_ref
    k_pages = refs[:pages]
    v_pages = refs[pages:2 * pages]
    ko_ref, vo_ref = refs[2 * pages:]
    nh = ko_ref.shape[-1] // k_pages[0].shape[-1]
    ps = k_pages[0].shape[0] // nh
    for src, dst in ((k_pages, ko_ref), (v_pages, vo_ref)):
        for j in range(pages):
            for pr in range(nh // 2):
                even = src[j][pl.ds(2 * pr, ps, stride=nh), :]
                odd = src[j][pl.ds(2 * pr + 1, ps, stride=nh), :]
                pair = jnp.concatenate([even, odd], axis=-1)
                dst[j * ps:(j + 1) * ps, pr * LANES:(pr + 1) * LANES] = pair.astype(BF16)


def _gather_cache(cache_k, cache_v, page_table):
    n, n_pages = page_table.shape
    n_pool, ps, nh, hd = cache_k.shape
    pages = min(GATHER_PAGES, n_pages)
    cache_k = cache_k.reshape(n_pool, ps * nh, hd)
    cache_v = cache_v.reshape(n_pool, ps * nh, hd)

    def page_spec(j):
        return pl.BlockSpec((None, ps * nh, hd), lambda b, s, pt: (pt[b * n_pages + s * pages + j], 0, 0))

    out_spec = pl.BlockSpec((None, pages * ps, nh * hd), lambda b, s, pt: (b, s, 0))
    out_sds = jax.ShapeDtypeStruct((n, n_pages * ps, nh * hd), BF16)
    grid_spec = pltpu.PrefetchScalarGridSpec(
        num_scalar_prefetch=1,
        grid=(n, n_pages // pages),
        in_specs=[page_spec(j) for j in range(pages)] + [page_spec(j) for j in range(pages)],
        out_specs=[out_spec, out_spec],
    )
    return pl.pallas_call(
        functools.partial(_gather_kernel, pages=pages),
        grid_spec=grid_spec,
        out_shape=[out_sds, out_sds],
        compiler_params=_params(("parallel", "parallel")),
        name="cache_gather",
    )(page_table.reshape(-1), *([cache_k] * pages), *([cache_v] * pages))


def _decode_kernel(q_ref, kn_ref, vn_ref, d_ref, k_ref, v_ref, o_ref):
    nh = d_ref.shape[0]
    d_b = q_ref.shape[-1]
    head_of_lane = lax.broadcasted_iota(jnp.int32, (nh, d_b), 1) // FOX_HD
    own = head_of_lane == lax.broadcasted_iota(jnp.int32, (nh, d_b), 0)
    q_rows = jnp.where(own, q_ref[...].astype(F32), 0.0).astype(BF16)
    k_new = kn_ref[...].astype(BF16).astype(F32)
    s_self = jnp.sum(q_rows.astype(F32) * k_new, axis=-1, keepdims=True)
    s = _dot_nt(q_rows, k_ref[...]) + d_ref[...]
    m = jnp.maximum(s_self, jnp.max(s, axis=-1, keepdims=True))
    p = jnp.exp2(s - m)
    p_self = jnp.exp2(s_self - m)
    denom = p_self + jnp.sum(p, axis=-1, keepdims=True)
    acc = p_self * vn_ref[...] + _dot(p.astype(BF16), v_ref[...])
    o_ref[...] = jnp.sum(jnp.where(own, acc / denom, 0.0), axis=0, keepdims=True)


def _decode_attn(q, k_new, v_new, bias_t, k_dense, v_dense):
    n, t, d_b = k_dense.shape
    nh = bias_t.shape[1]
    row_spec = pl.BlockSpec((None, 1, d_b), lambda b: (b, 0, 0))
    kv_spec = pl.BlockSpec((None, t, d_b), lambda b: (b, 0, 0))
    return pl.pallas_call(
        _decode_kernel,
        grid=(n,),
        in_specs=[row_spec, row_spec, row_spec, pl.BlockSpec((None, nh, t), lambda b: (b, 0, 0)), kv_spec, kv_spec],
        out_specs=row_spec,
        out_shape=jax.ShapeDtypeStruct((n, 1, d_b), F32),
        compiler_params=_params(("parallel",)),
        name="fox_decode_attn",
    )(q, k_new, v_new, bias_t, k_dense, v_dense)


def kernel(x_prompt, x_sample, c_prompt, c_sample, state_gla, cache_k, cache_v, cache_logf, page_table,
           ada_w, ada_b, gla_w_in, gla_w_g2, gla_b_g, gla_onorm, gla_w_out, kv_ada_w, kv_ada_b, w_kv, b_f,
           k_norm, fox_w_in, q_norm, fox_w_out):
    bp, seq, d = x_prompt.shape
    ns = x_sample.shape[0]
    n_a = gla_w_in.shape[0]
    depth = ada_w.shape[0]
    nh = b_f.shape[0]
    d_b = nh * FOX_HD
    dk, dv = state_gla.shape[-2:]

    pad = (-(ns + bp)) % 8
    c_all = jnp.concatenate([c_sample, c_prompt, jnp.zeros((pad, d), F32)], axis=0)
    mod = _ada(c_all, ada_w, ada_b)
    kv_mod = _ada(c_all, kv_ada_w[None], kv_ada_b[None])[0]

    def split_mod(m, n_parts):
        parts = jnp.split(m, n_parts, axis=-1)
        return ([p[ns:ns + bp, None, :] for p in parts],
                [p[None, :ns, :] for p in parts])

    xp = x_prompt
    xs = x_sample.reshape(1, ns, d)
    sp_states, ss_states = [], None
    for layer in range(n_a):
        mod_p, mod_s = split_mod(mod[layer], 3)
        q, k, v, g, la = _gla_in(xp, mod_p[0], mod_p[1], gla_w_in[layer], gla_w_g2[layer], gla_b_g[layer])
        o, s_fin = _gla_scan(q, k, v, la)
        sp_states.append(s_fin)
        xp = _out_proj(o, g, xp, mod_p[2], gla_w_out[layer], gla_onorm[layer])
        q, k, v, g, la = _gla_in(xs, mod_s[0], mod_s[1], gla_w_in[layer], gla_w_g2[layer], gla_b_g[layer])
        o, ss_states = _gla_step(q[0], k[0], la[0], v[0], state_gla[layer], ss_states)
        xs = _out_proj(o.reshape(1, ns, -1), g, xs, mod_s[2], gla_w_out[layer], gla_onorm[layer])

    kvm_p, kvm_s = split_mod(kv_mod, 2)
    k_p, v_p, lf_p, kb_p, vb_p = _kv_proj(xp, kvm_p[0], kvm_p[1], w_kv, b_f, k_norm)
    k_s, v_s, lf_s, _, _ = _kv_proj(xs, kvm_s[0], kvm_s[1], w_kv, b_f, k_norm)
    f2_p = _cumsum_rows(lf_p)
    ka_p = _key_aug(kb_p, f2_p)
    vt_p, fq_p = _fox_attn_layouts(vb_p, f2_p, min(ATTN_TK, seq))
    lf_new = lf_s.reshape(ns, 1, nh)
    bias_t = jnp.swapaxes(_forget_suffix(cache_logf, page_table, lf_new), 1, 2)
    k_dense, v_dense = _gather_cache(cache_k, cache_v, page_table)
    k_new = k_s.reshape(ns, 1, d_b)
    v_new = v_s.reshape(ns, 1, d_b)

    for j in range(depth - n_a):
        mod_p, mod_s = split_mod(mod[n_a + j], 3)
        q, g = _fox_in(xp, mod_p[0], mod_p[1], fox_w_in[j], q_norm[j])
        o = _fox_attn(q, ka_p, vt_p, fq_p)
        xp = _out_proj(o, g, xp, mod_p[2], fox_w_out[j])
        q, g = _fox_in(xs, mod_s[0], mod_s[1], fox_w_in[j], q_norm[j])
        o = _decode_attn(q.reshape(ns, 1, d_b), k_new, v_new, bias_t, k_dense, v_dense)
        xs = _out_proj(o.reshape(1, ns, d_b), g, xs, mod_s[2], fox_w_out[j])

    return (xp, xs.reshape(ns, 1, d),
            jnp.stack(sp_states), ss_states,
            k_p, v_p, lf_p,
            k_s.reshape(ns, 1, nh, FOX_HD), v_s.reshape(ns, 1, nh, FOX_HD), lf_new)
```

```python
import functools

import jax
import jax.numpy as jnp
from jax import lax
from jax.experimental import pallas as pl
from jax.experimental.pallas import tpu as pltpu

F32 = jnp.float32
BF16 = jnp.bfloat16

EPS = 1e-6
GATE_TAU = 16.0
GLA_HEADS = 4
GLA_CHUNK = 64
GLA_SUB = 16
FOX_HD = 64
LANES = 128
NEG_BIG = -1e30
LOG2E = 1.4426950408889634
VMEM_LIMIT = 56 * 1024 * 1024

ROW_TILE = 512
ATTN_TK = 512
SCAN_T = 512
GLA_STEP_G = 4


def _params(sem):
    return pltpu.CompilerParams(dimension_semantics=sem, vmem_limit_bytes=VMEM_LIMIT)


def _silu(x):
    return x / (1.0 + jnp.exp(-x))


def _log_sigmoid(x):
    return jnp.minimum(x, 0.0) - jnp.log1p(jnp.exp(-jnp.abs(x)))


def _rms(x):
    return x * lax.rsqrt(jnp.mean(x * x, axis=-1, keepdims=True) + EPS)


def _dot(a, b):
    return jnp.dot(a, b, preferred_element_type=F32)


def _dot_nt(a, b):
    return lax.dot_general(a, b, (((1,), (1,)), ((), ())), preferred_element_type=F32)


def _dot_tn(a, b):
    return lax.dot_general(a, b, (((0,), (0,)), ((), ())), preferred_element_type=F32)


def _split3(x):
    p0 = x.astype(BF16)
    r1 = x - p0.astype(F32)
    p1 = r1.astype(BF16)
    p2 = (r1 - p1.astype(F32)).astype(BF16)
    return p0, p1, p2


def _exact_left_mul(mat01, x):
    p0, p1, p2 = _split3(x)
    return _dot(mat01, p0) + _dot(mat01, p1) + _dot(mat01, p2)


def _modulated_norm(x_ref, shift_ref, scale_ref):
    h = _rms(x_ref[...]) * (1.0 + scale_ref[...]) + shift_ref[...]
    return h.astype(BF16)


def _head_rms64(p):
    lane = lax.broadcasted_iota(jnp.int32, (1, LANES), 1)
    low = lane < FOX_HD
    outs = []
    for c in range(p.shape[-1] // LANES):
        blk = p[:, c * LANES:(c + 1) * LANES]
        sq = blk * blk
        s_lo = jnp.sum(jnp.where(low, sq, 0.0), axis=-1, keepdims=True)
        s_hi = jnp.sum(jnp.where(low, 0.0, sq), axis=-1, keepdims=True)
        r_lo = lax.rsqrt(s_lo * (1.0 / FOX_HD) + EPS)
        r_hi = lax.rsqrt(s_hi * (1.0 / FOX_HD) + EPS)
        outs.append(blk * jnp.where(low, r_lo, r_hi))
    return jnp.concatenate(outs, axis=-1)


def _ada_kernel(c_ref, w_ref, b_ref, o_ref):
    a = _silu(c_ref[...]).astype(BF16)
    o_ref[...] = _dot(a, w_ref[...].astype(BF16)) + b_ref[...]


def _ada(c_all, w, b, tn=1024):
    nl, d, n = w.shape
    rows = c_all.shape[0]
    return pl.pallas_call(
        _ada_kernel,
        grid=(nl, n // tn),
        in_specs=[
            pl.BlockSpec((rows, d), lambda l, j: (0, 0)),
            pl.BlockSpec((None, d, tn), lambda l, j: (l, 0, j)),
            pl.BlockSpec((None, 1, tn), lambda l, j: (l, 0, j)),
        ],
        out_specs=pl.BlockSpec((None, rows, tn), lambda l, j: (l, 0, j)),
        out_shape=jax.ShapeDtypeStruct((nl, rows, n), F32),
        compiler_params=_params(("parallel", "parallel")),
        name="ada_mod",
    )(c_all, w, b.reshape(nl, 1, n))


def _row_spec(tm, width):
    return pl.BlockSpec((None, tm, width), lambda b, i: (b, i, 0))


def _mod_spec(mod, tm):
    if mod.shape[1] == 1:
        return pl.BlockSpec((None, 1, mod.shape[2]), lambda b, i: (b, 0, 0))
    return pl.BlockSpec((None, tm, mod.shape[2]), lambda b, i: (b, i, 0))


def _full_spec(a):
    nd = a.ndim
    return pl.BlockSpec(a.shape, lambda b, i: (0,) * nd)


def _row_tile(length):
    return min(ROW_TILE, length)


def _gla_in_kernel(x_ref, sh_ref, sc_ref, wq_ref, wk_ref, wv_ref, wg_ref, wz_ref, wg2_ref, bg_ref,
                   q_ref, k_ref, v_ref, g_ref, la_ref, *, q_scale):
    h = _modulated_norm(x_ref, sh_ref, sc_ref)
    q_ref[...] = _dot(h, wq_ref[...]) * q_scale
    k_ref[...] = _dot(h, wk_ref[...])
    v_ref[...] = _dot(h, wv_ref[...])
    g_ref[...] = _dot(h, wg_ref[...])
    z = _dot(h, wz_ref[...])
    t = _dot(z.astype(BF16), wg2_ref[...]) + bg_ref[...]
    la_ref[...] = _log_sigmoid(t) * (1.0 / GATE_TAU)


def _gla_in(x, shift, scale, w_in, w_g2, b_g):
    bsz, length, d = x.shape
    dk_tot = w_g2.shape[1]
    rank = w_g2.shape[0]
    dv_tot = (w_in.shape[1] - 2 * dk_tot - rank) // 2
    dk = dk_tot // GLA_HEADS
    wb = w_in.astype(BF16)
    wq = wb[:, :dk_tot]
    wk = wb[:, dk_tot:2 * dk_tot]
    wv = wb[:, 2 * dk_tot:2 * dk_tot + dv_tot]
    wg = wb[:, 2 * dk_tot + dv_tot:2 * dk_tot + 2 * dv_tot]
    wz = wb[:, 2 * dk_tot + 2 * dv_tot:]
    wg2 = w_g2.astype(BF16)
    bg = b_g.reshape(1, dk_tot)
    tm = _row_tile(length)
    weights = (wq, wk, wv, wg, wz, wg2, bg)
    widths = (dk_tot, dk_tot, dv_tot, dv_tot, dk_tot)
    return pl.pallas_call(
        functools.partial(_gla_in_kernel, q_scale=dk ** -0.5),
        grid=(bsz, length // tm),
        in_specs=[_row_spec(tm, d), _mod_spec(shift, tm), _mod_spec(scale, tm)]
        + [_full_spec(w) for w in weights],
        out_specs=[_row_spec(tm, w) for w in widths],
        out_shape=[jax.ShapeDtypeStruct((bsz, length, w), F32) for w in widths],
        compiler_params=_params(("parallel", "parallel")),
        name="gla_in_proj",
    )(x, shift, scale, *weights)


def _gla_chunk_head(q, k, v, b, s_old, rows):
    c, dk = q.shape
    nsub = c // GLA_SUB
    b_last = b[c - 1:c, :]
    o = _dot((q * jnp.exp(b)).astype(BF16), s_old.astype(BF16))
    kd = k * jnp.exp(b_last - b)
    upd = _dot_tn(kd.astype(BF16), v.astype(BF16))
    decay_col = jnp.exp(jnp.transpose(jnp.broadcast_to(b_last, (dk, dk))))
    s_new = jnp.concatenate([decay_col] * (s_old.shape[1] // dk), axis=1) * s_old + upd
    qcat, kcat = [], []
    for j in range(nsub - 1):
        r_j = b[(j + 1) * GLA_SUB - 1:(j + 1) * GLA_SUB, :]
        in_j = (rows >= j * GLA_SUB) & (rows < (j + 1) * GLA_SUB)
        after_j = rows >= (j + 1) * GLA_SUB
        kcat.append(jnp.where(in_j, k * jnp.exp(jnp.where(in_j, r_j - b, 0.0)), 0.0))
        qcat.append(jnp.where(after_j, q * jnp.exp(jnp.where(after_j, b - r_j, 0.0)), 0.0))
    att = _dot_nt(jnp.concatenate(qcat, axis=1).astype(BF16), jnp.concatenate(kcat, axis=1).astype(BF16))
    sub_row = lax.broadcasted_iota(jnp.int32, (GLA_SUB, 1), 0)
    col_id = lax.broadcasted_iota(jnp.int32, (1, c), 1)
    diag_blocks = []
    for j in range(nsub):
        lo = j * GLA_SUB
        qj = q[lo:lo + GLA_SUB, :]
        bj = b[lo:lo + GLA_SUB, :]
        blk = jnp.zeros((GLA_SUB, c), F32)
        for t in range(GLA_SUB):
            src = lo + t
            valid = sub_row >= t
            e = jnp.exp(jnp.where(valid, bj - b[src:src + 1, :], 0.0))
            col = jnp.sum(qj * k[src:src + 1, :] * e, axis=-1, keepdims=True)
            blk = jnp.where((col_id == src) & valid, col, blk)
        diag_blocks.append(blk)
    att = att + jnp.concatenate(diag_blocks, axis=0)
    o = o + _dot(att.astype(BF16), v.astype(BF16))
    return o, s_new


def _gla_scan_kernel(q_ref, k_ref, v_ref, la_ref, o_ref, sfin_ref, s_sc, *, n_chunks):
    t = pl.program_id(1)
    dk = q_ref.shape[-1] // GLA_HEADS
    dv = v_ref.shape[-1] // GLA_HEADS
    c = GLA_CHUNK

    @pl.when(t == 0)
    def _():
        s_sc[...] = jnp.zeros_like(s_sc)

    tri = (lax.broadcasted_iota(jnp.int32, (c, c), 0) >= lax.broadcasted_iota(jnp.int32, (c, c), 1))
    tri = jnp.where(tri, 1.0, 0.0).astype(BF16)
    rows = lax.broadcasted_iota(jnp.int32, (c, 1), 0)

    def chunk(ci, carry):
        r0 = pl.multiple_of(ci * c, c)
        b_all = _exact_left_mul(tri, la_ref[pl.ds(r0, c), :])
        for h in range(GLA_HEADS):
            o, s_new = _gla_chunk_head(
                q_ref[pl.ds(r0, c), h * dk:(h + 1) * dk],
                k_ref[pl.ds(r0, c), h * dk:(h + 1) * dk],
                v_ref[pl.ds(r0, c), h * dv:(h + 1) * dv],
                b_all[:, h * dk:(h + 1) * dk],
                s_sc[h], rows)
            o_ref[pl.ds(r0, c), h * dv:(h + 1) * dv] = o
            s_sc[h] = s_new
        return carry

    lax.fori_loop(0, n_chunks, chunk, 0)

    @pl.when(t == pl.num_programs(1) - 1)
    def _():
        sfin_ref[...] = s_sc[...]


def _gla_scan(q, k, v, la):
    bsz, length, dk_tot = q.shape
    dv_tot = v.shape[-1]
    dk, dv = dk_tot // GLA_HEADS, dv_tot // GLA_HEADS
    tt = min(SCAN_T, length)
    return pl.pallas_call(
        functools.partial(_gla_scan_kernel, n_chunks=tt // GLA_CHUNK),
        grid=(bsz, length // tt),
        in_specs=[_row_spec(tt, dk_tot), _row_spec(tt, dk_tot), _row_spec(tt, dv_tot), _row_spec(tt, dk_tot)],
        out_specs=[_row_spec(tt, dv_tot),
                   pl.BlockSpec((None, GLA_HEADS, dk, dv), lambda b, i: (b, 0, 0, 0))],
        out_shape=[jax.ShapeDtypeStruct((bsz, length, dv_tot), F32),
                   jax.ShapeDtypeStruct((bsz, GLA_HEADS, dk, dv), F32)],
        scratch_shapes=[pltpu.VMEM((GLA_HEADS, dk, dv), F32)],
        compiler_params=_params(("parallel", "arbitrary")),
        name="gla_scan",
    )(q, k, v, la)


def _gla_step_kernel(qt_ref, kt_ref, lat_ref, v_ref, s_ref, *rest, n_prev):
    if n_prev:
        prev_ref, o_ref, so_ref = rest
        so_ref[0:n_prev] = prev_ref[...]
    else:
        o_ref, so_ref = rest
    dk, dv = s_ref.shape[-2:]
    for g in range(s_ref.shape[0]):
        for h in range(GLA_HEADS):
            q_col = qt_ref[h * dk:(h + 1) * dk, g:g + 1]
            k_col = kt_ref[h * dk:(h + 1) * dk, g:g + 1]
            a_col = jnp.exp(lat_ref[h * dk:(h + 1) * dk, g:g + 1])
            v_row = v_ref[g:g + 1, h * dv:(h + 1) * dv]
            s_new = a_col * s_ref[g, h] + k_col * v_row
            so_ref[n_prev, g, h] = s_new
            o_ref[g:g + 1, h * dv:(h + 1) * dv] = jnp.sum(q_col * s_new, axis=0, keepdims=True)


def _gla_step(q, k, la, v, states, layer, prev=None):
    n, dk_tot = q.shape
    dv_tot = v.shape[-1]
    dk, dv = states.shape[-2:]
    n_prev = 0 if prev is None else prev.shape[0]
    g = GLA_STEP_G
    steps = n // g

    def cols(a):
        return a.reshape(steps, g, a.shape[-1]).transpose(0, 2, 1)

    col_spec = pl.BlockSpec((None, dk_tot, g), lambda i: (i, 0, 0))
    prev_in = [] if prev is None else [prev]
    prev_spec = [] if prev is None else [pl.BlockSpec((n_prev, g, GLA_HEADS, dk, dv), lambda i: (0, i, 0, 0, 0))]
    o, stacked = pl.pallas_call(
        functools.partial(_gla_step_kernel, n_prev=n_prev),
        grid=(steps,),
        in_specs=[col_spec, col_spec, col_spec,
                  pl.BlockSpec((None, g, dv_tot), lambda i: (i, 0, 0)),
                  pl.BlockSpec((None, g, GLA_HEADS, dk, dv), lambda i: (layer, i, 0, 0, 0))] + prev_spec,
        out_specs=[pl.BlockSpec((None, g, dv_tot), lambda i: (i, 0, 0)),
                   pl.BlockSpec((n_prev + 1, g, GLA_HEADS, dk, dv), lambda i: (0, i, 0, 0, 0))],
        out_shape=[jax.ShapeDtypeStruct((steps, g, dv_tot), F32),
                   jax.ShapeDtypeStruct((n_prev + 1,) + states.shape[1:], F32)],
        compiler_params=_params(("parallel",)),
        name="gla_step",
    )(cols(q), cols(k), cols(la), v.reshape(steps, g, dv_tot), states, *prev_in)
    return o.reshape(n, dv_tot), stacked


def _out_kernel(o_ref, g_ref, x_ref, gate_ref, w_ref, *rest, head_dim):
    if head_dim:
        onorm_ref, y_ref = rest
        o = o_ref[...]
        o = jnp.concatenate(
            [_rms(o[:, h * head_dim:(h + 1) * head_dim]) for h in range(o.shape[-1] // head_dim)], axis=-1)
        o = o * onorm_ref[...]
    else:
        (y_ref,) = rest
        o = o_ref[...]
    a = (o * _silu(g_ref[...])).astype(BF16)
    y_ref[...] = x_ref[...] + gate_ref[...] * _dot(a, w_ref[...])


def _out_proj(o, g, x, gate, w_out, onorm=None):
    bsz, length, d = x.shape
    width = o.shape[-1]
    tm = _row_tile(length)
    wb = w_out.astype(BF16)
    extra, head_dim = [], 0
    if onorm is not None:
        head_dim = onorm.shape[0]
        extra = [jnp.tile(onorm, width // head_dim).reshape(1, width)]
    return pl.pallas_call(
        functools.partial(_out_kernel, head_dim=head_dim),
        grid=(bsz, length // tm),
        in_specs=[_row_spec(tm, width), _row_spec(tm, width), _row_spec(tm, d), _mod_spec(gate, tm),
                  _full_spec(wb)] + [_full_spec(e) for e in extra],
        out_specs=_row_spec(tm, d),
        out_shape=jax.ShapeDtypeStruct((bsz, length, d), F32),
        compiler_params=_params(("parallel", "parallel")),
        name="out_proj",
    )(o, g, x, gate, wb, *extra)


def _kv_kernel(x_ref, sh_ref, sc_ref, wk_ref, wv_ref, wf_ref, wft_ref, bf_ref, bft_ref, kn_ref,
               kt_ref, vt_ref, lft_ref, lf_ref, kb_ref, vtb_ref):
    h = _modulated_norm(x_ref, sh_ref, sc_ref)
    k = _head_rms64(_dot(h, wk_ref[...])) * kn_ref[...]
    v = _dot(h, wv_ref[...])
    k_t = jnp.transpose(k)
    v_t = jnp.transpose(v)
    for hd in range(kt_ref.shape[0]):
        kt_ref[hd] = k_t[hd * FOX_HD:(hd + 1) * FOX_HD, :]
        vt_ref[hd] = v_t[hd * FOX_HD:(hd + 1) * FOX_HD, :]
    kb_ref[...] = k.astype(BF16)
    vtb_ref[...] = v_t.astype(BF16)
    lf_ref[...] = _log_sigmoid(_dot(h, wf_ref[...]) + bf_ref[...])
    lft_ref[...] = _log_sigmoid(_dot_nt(wft_ref[...], h) + bft_ref[...])


def _kv_proj(x, shift, scale, w_kv, b_f, k_norm):
    bsz, length, d = x.shape
    nh = b_f.shape[0]
    d_b = (w_kv.shape[1] - nh) // 2
    hd = d_b // nh
    wb = w_kv.astype(BF16)
    weights = (wb[:, :d_b], wb[:, d_b:2 * d_b], wb[:, 2 * d_b:], wb[:, 2 * d_b:].T,
               b_f.reshape(1, nh), b_f.reshape(nh, 1), jnp.tile(k_norm, nh).reshape(1, d_b))
    tm = _row_tile(length)
    tok_minor = pl.BlockSpec((None, nh, hd, tm), lambda b, i: (b, 0, 0, i))
    return pl.pallas_call(
        _kv_kernel,
        grid=(bsz, length // tm),
        in_specs=[_row_spec(tm, d), _mod_spec(shift, tm), _mod_spec(scale, tm)]
        + [_full_spec(w) for w in weights],
        out_specs=[tok_minor, tok_minor,
                   pl.BlockSpec((None, nh, tm), lambda b, i: (b, 0, i)),
                   _row_spec(tm, nh), _row_spec(tm, d_b),
                   pl.BlockSpec((None, d_b, tm), lambda b, i: (b, 0, i))],
        out_shape=[jax.ShapeDtypeStruct((bsz, nh, hd, length), F32),
                   jax.ShapeDtypeStruct((bsz, nh, hd, length), F32),
                   jax.ShapeDtypeStruct((bsz, nh, length), F32),
                   jax.ShapeDtypeStruct((bsz, length, nh), F32),
                   jax.ShapeDtypeStruct((bsz, length, d_b), BF16),
                   jax.ShapeDtypeStruct((bsz, d_b, length), BF16)],
        compiler_params=_params(("parallel", "parallel")),
        name="kv_proj",
    )(x, shift, scale, *weights)


def _fox_in_kernel(x_ref, sh_ref, sc_ref, wq_ref, wg_ref, qn_ref, q_ref, g_ref):
    h = _modulated_norm(x_ref, sh_ref, sc_ref)
    q_ref[...] = (_head_rms64(_dot(h, wq_ref[...])) * qn_ref[...]).astype(BF16)
    g_ref[...] = _dot(h, wg_ref[...])


def _fox_in(x, shift, scale, w_in, q_norm):
    bsz, length, d = x.shape
    d_b = w_in.shape[1] // 2
    wb = w_in.astype(BF16)
    qn = (jnp.tile(q_norm, d_b // FOX_HD) * (FOX_HD ** -0.5 * LOG2E)).reshape(1, d_b)
    weights = (wb[:, :d_b], wb[:, d_b:], qn)
    tm = _row_tile(length)
    return pl.pallas_call(
        _fox_in_kernel,
        grid=(bsz, length // tm),
        in_specs=[_row_spec(tm, d), _mod_spec(shift, tm), _mod_spec(scale, tm)]
        + [_full_spec(w) for w in weights],
        out_specs=[_row_spec(tm, d_b), _row_spec(tm, d_b)],
        out_shape=[jax.ShapeDtypeStruct((bsz, length, d_b), BF16),
                   jax.ShapeDtypeStruct((bsz, length, d_b), F32)],
        compiler_params=_params(("parallel", "parallel")),
        name="fox_in_proj",
    )(x, shift, scale, *weights)


def _cumsum_kernel(lf_ref, f_ref, carry_sc):
    r = lf_ref.shape[0]

    @pl.when(pl.program_id(1) == 0)
    def _():
        carry_sc[...] = jnp.zeros_like(carry_sc)

    tri = lax.broadcasted_iota(jnp.int32, (r, r), 0) >= lax.broadcasted_iota(jnp.int32, (r, r), 1)
    tri = jnp.where(tri, 1.0, 0.0).astype(BF16)
    f = _exact_left_mul(tri, lf_ref[...]) + carry_sc[...]
    f_ref[...] = f * LOG2E
    carry_sc[...] = f[r - 1:r, :]


def _cumsum_rows(lf):
    bsz, length, nh = lf.shape
    r = min(ROW_TILE, length)
    return pl.pallas_call(
        _cumsum_kernel,
        grid=(bsz, length // r),
        in_specs=[_row_spec(r, nh)],
        out_specs=_row_spec(r, nh),
        out_shape=jax.ShapeDtypeStruct(lf.shape, F32),
        scratch_shapes=[pltpu.VMEM((1, nh), F32)],
        compiler_params=_params(("parallel", "arbitrary")),
        name="forget_cumsum",
    )(lf)


N_PIECES = 3


def _key_aug_kernel(kb_ref, f_ref, place_k_ref, place_f_ref, ones_ref, ka_ref):
    out = _dot(kb_ref[...], place_k_ref[...]) + ones_ref[...]
    for j, piece in enumerate(_split3(f_ref[...])):
        out = out - _dot(piece, place_f_ref[j])
    for h in range(ka_ref.shape[0]):
        ka_ref[h] = out[:, h * LANES:(h + 1) * LANES].astype(BF16)


def _key_aug(kb, f2):
    bsz, length, d_b = kb.shape
    nh = f2.shape[-1]
    tm = _row_tile(length)
    col = jnp.arange(nh * LANES)
    row = jnp.arange(d_b)
    place_k = ((col[None, :] // LANES == row[:, None] // FOX_HD)
               & (col[None, :] % LANES == row[:, None] % FOX_HD)).astype(BF16)
    head = jnp.arange(nh)
    place_f = jnp.stack([(col[None, :] == head[:, None] * LANES + FOX_HD + j) for j in range(N_PIECES)]
                        ).astype(BF16)
    ones = ((col % LANES >= FOX_HD + N_PIECES) & (col % LANES < FOX_HD + 2 * N_PIECES)).astype(F32)[None, :]
    return pl.pallas_call(
        _key_aug_kernel,
        grid=(bsz, length // tm),
        in_specs=[_row_spec(tm, d_b), _row_spec(tm, nh), _full_spec(place_k), _full_spec(place_f),
                  _full_spec(ones)],
        out_specs=pl.BlockSpec((None, nh, tm, LANES), lambda b, i: (b, 0, i, 0)),
        out_shape=jax.ShapeDtypeStruct((bsz, nh, length, LANES), BF16),
        compiler_params=_params(("parallel", "parallel")),
        name="fox_key_aug",
    )(kb, f2, place_k, place_f, ones)


def _fox_attn_kernel(q_ref, ka_ref, vt_ref, fq_ref, o_ref, qt_sc, m_sc, l_sc, acc_sc):
    qi = pl.program_id(2)
    tq = q_ref.shape[0]
    tk = tq
    q_t = jnp.transpose(q_ref[...].astype(F32))
    brow = lax.broadcasted_iota(jnp.int32, (FOX_HD, 1), 0)
    ones_rows = jnp.broadcast_to(jnp.where(brow < N_PIECES, 1.0, 0.0), (FOX_HD, tq))

    def q_aug(i, c):
        extra = ones_rows
        if c is not None:
            for j, piece in enumerate(_split3(c)):
                extra = jnp.where(brow == N_PIECES + j, piece.astype(F32), extra)
        return jnp.concatenate([q_t[i * FOX_HD:(i + 1) * FOX_HD, :], extra], axis=0).astype(BF16)

    q_plain = (q_aug(0, None), q_aug(1, None))

    def reset():
        m_sc[...] = jnp.full_like(m_sc, NEG_BIG)
        l_sc[...] = jnp.zeros_like(l_sc)
        acc_sc[...] = jnp.zeros_like(acc_sc)

    def online_block(kj, causal):
        c0 = pl.multiple_of(kj * tk, tk)
        ts = []
        for i in range(2):
            t = _dot(ka_ref[i, pl.ds(c0, tk), :], q_plain[i])
            if causal:
                key = lax.broadcasted_iota(jnp.int32, (tk, tq), 0)
                qry = lax.broadcasted_iota(jnp.int32, (tk, tq), 1)
                t = jnp.where(key <= qry, t, NEG_BIG)
            ts.append(t)
        ps, alphas = [], []
        for i in range(2):
            fq = fq_ref[i:i + 1, :]
            m_old = m_sc[i]
            m_new = jnp.maximum(m_old, jnp.max(ts[i], axis=0, keepdims=True) + fq)
            alpha = jnp.exp2(m_old - m_new)
            p = jnp.exp2(ts[i] + (fq - m_new))
            l_sc[i] = alpha * l_sc[i] + jnp.sum(p, axis=0, keepdims=True)
            m_sc[i] = m_new
            ps.append(p.astype(BF16))
            alphas.append(alpha)
        for i in range(2):
            acc_sc[i] = alphas[i] * acc_sc[i] + _dot(vt_ref[i * FOX_HD:(i + 1) * FOX_HD, pl.ds(c0, tk)], ps[i])

    def result():
        return jnp.concatenate([acc_sc[0] / l_sc[0], acc_sc[1] / l_sc[1]], axis=0)

    reset()
    online_block(qi, True)
    for i in range(2):
        qt_sc[i] = q_aug(i, fq_ref[i:i + 1, :] - m_sc[i])

    def fixed_shift_blocks(kjs):
        ps, sums = [], [0.0, 0.0]
        for kj in kjs:
            c0 = pl.multiple_of(kj * tk, tk)
            for i in range(2):
                p = jnp.exp2(_dot(ka_ref[i, pl.ds(c0, tk), :], qt_sc[i]))
                sums[i] = sums[i] + jnp.sum(p, axis=0, keepdims=True)
                ps.append(p.astype(BF16))
        for i in range(2):
            l_sc[i] = l_sc[i] + sums[i]
            pv = 0.0
            for n, kj in enumerate(kjs):
                c0 = pl.multiple_of(kj * tk, tk)
                pv = pv + _dot(vt_ref[i * FOX_HD:(i + 1) * FOX_HD, pl.ds(c0, tk)], ps[2 * n + i])
            acc_sc[i] = acc_sc[i] + pv

    def quad_body(j, carry):
        fixed_shift_blocks(tuple(4 * j + n for n in range(4)))
        return carry

    lax.fori_loop(0, qi // 4, quad_body, 0)
    done = (qi // 4) * 4

    @pl.when(qi % 4 >= 2)
    def _():
        fixed_shift_blocks((done, done + 1))

    @pl.when(qi % 2 == 1)
    def _():
        fixed_shift_blocks((qi - 1,))
    o_t = result()
    bad = (jnp.sum(jnp.where(jnp.isfinite(o_t), 0.0, 1.0))
           + jnp.sum(jnp.where(jnp.isfinite(l_sc[...]), 0.0, 1.0)))

    @pl.when(bad == 0.0)
    def _():
        o_ref[...] = jnp.transpose(o_t)

    @pl.when(bad != 0.0)
    def _():
        reset()

        def body(kj, carry):
            online_block(kj, False)
            return carry

        lax.fori_loop(0, qi, body, 0)
        online_block(qi, True)
        o_ref[...] = jnp.transpose(result())


def _fox_attn(q, ka, vt, fq):
    bsz, length, d_b = q.shape
    pairs = fq.shape[1]
    tq = tk = min(ATTN_TK, length)
    return pl.pallas_call(
        _fox_attn_kernel,
        grid=(bsz, pairs, length // tq),
        in_specs=[
            pl.BlockSpec((None, tq, LANES), lambda b, p, i: (b, i, p)),
            pl.BlockSpec((None, 2, length, LANES), lambda b, p, i: (b, p, 0, 0)),
            pl.BlockSpec((None, LANES, length), lambda b, p, i: (b, p, 0)),
            pl.BlockSpec((None, None, 2, tq), lambda b, p, i: (b, p, 0, i)),
        ],
        out_specs=pl.BlockSpec((None, tq, LANES), lambda b, p, i: (b, i, p)),
        out_shape=jax.ShapeDtypeStruct((bsz, length, d_b), F32),
        scratch_shapes=[pltpu.VMEM((2, LANES, tq), BF16),
                        pltpu.VMEM((2, 1, tq), F32), pltpu.VMEM((2, 1, tq), F32),
                        pltpu.VMEM((2, FOX_HD, tq), F32)],
        compiler_params=_params(("parallel", "parallel", "arbitrary")),
        name="fox_attn",
    )(q, ka, vt, fq)


def _decode_kernel(pt_ref, *refs, n_pages):
    del pt_ref
    k_pages = refs[:n_pages]
    v_pages = refs[n_pages:2 * n_pages]
    f_pages = refs[2 * n_pages:3 * n_pages]
    qt_ref, q_ref, kn_ref, vnt_ref, lfn_ref, o_ref, s_sc = refs[3 * n_pages:]
    nh, hd, ps = k_pages[0].shape

    later = lax.broadcasted_iota(jnp.int32, (ps, ps), 0) > lax.broadcasted_iota(jnp.int32, (ps, ps), 1)
    later = jnp.where(later, 1.0, 0.0).astype(BF16)
    carry = lfn_ref[...]
    for j in reversed(range(n_pages)):
        page = f_pages[j][...]
        within = sum(_dot(piece, later) for piece in _split3(page))
        s_sc[:, j * ps:(j + 1) * ps] = (within + carry) * LOG2E
        carry = carry + jnp.sum(page, axis=-1, keepdims=True)

    for h in range(nh):
        q_col = jnp.broadcast_to(qt_ref[:, h:h + 1].astype(F32), (hd, ps))
        for j in range(n_pages):
            cols = slice(j * ps, (j + 1) * ps)
            s_sc[h:h + 1, cols] = s_sc[h:h + 1, cols] + jnp.sum(k_pages[j][h] * q_col, axis=0, keepdims=True)
    s_self = jnp.sum(q_ref[...].astype(F32) * kn_ref[...], axis=-1, keepdims=True)
    s = s_sc[...]
    m = jnp.maximum(s_self, jnp.max(s, axis=-1, keepdims=True))
    p = jnp.exp2(s - m)
    p_self = jnp.exp2(s_self - m)
    denom = p_self + jnp.sum(p, axis=-1, keepdims=True)
    s_sc[...] = p

    head_lane = lax.broadcasted_iota(jnp.int32, (1, nh), 1)
    o_t = jnp.zeros((hd, nh), F32)
    for h in range(nh):
        acc = jnp.zeros((hd, ps), F32)
        for j in range(n_pages):
            acc = acc + v_pages[j][h] * s_sc[h:h + 1, j * ps:(j + 1) * ps]
        o_t = jnp.where(head_lane == h, jnp.sum(acc, axis=-1, keepdims=True), o_t)

    def as_row(col):
        eye = lax.broadcasted_iota(jnp.int32, (nh, nh), 0) == lax.broadcasted_iota(jnp.int32, (nh, nh), 1)
        return jnp.sum(jnp.where(eye, col, 0.0), axis=0, keepdims=True)

    o_ref[...] = (o_t + vnt_ref[...] * as_row(p_self)) / as_row(denom)


def _decode_attn(q, k_new, v_new, lf_new, cache_k, cache_v, cache_logf, page_table):
    n, n_pages = page_table.shape
    n_pool, ps, nh, hd = cache_k.shape
    del n_pool
    k_view = cache_k.transpose(0, 2, 3, 1)
    v_view = cache_v.transpose(0, 2, 3, 1)
    f_view = cache_logf.transpose(0, 2, 1)

    def kv_spec(j):
        return pl.BlockSpec((None, nh, hd, ps), lambda b, pt: (pt[b * n_pages + j], 0, 0, 0))

    def f_spec(j):
        return pl.BlockSpec((None, nh, ps), lambda b, pt: (pt[b * n_pages + j], 0, 0))

    t_spec = pl.BlockSpec((None, hd, nh), lambda b, pt: (b, 0, 0))
    r_spec = pl.BlockSpec((None, nh, hd), lambda b, pt: (b, 0, 0))
    grid_spec = pltpu.PrefetchScalarGridSpec(
        num_scalar_prefetch=1,
        grid=(n,),
        in_specs=[kv_spec(j) for j in range(n_pages)] + [kv_spec(j) for j in range(n_pages)]
        + [f_spec(j) for j in range(n_pages)]
        + [t_spec, r_spec, r_spec, t_spec, pl.BlockSpec((None, nh, 1), lambda b, pt: (b, 0, 0))],
        out_specs=t_spec,
        scratch_shapes=[pltpu.VMEM((nh, n_pages * ps), F32)],
    )
    o_t = pl.pallas_call(
        functools.partial(_decode_kernel, n_pages=n_pages),
        grid_spec=grid_spec,
        out_shape=jax.ShapeDtypeStruct((n, hd, nh), F32),
        compiler_params=_params(("parallel",)),
        name="fox_decode_attn",
    )(page_table.reshape(-1), *([k_view] * n_pages), *([v_view] * n_pages), *([f_view] * n_pages),
      q.transpose(0, 2, 1), q, k_new, v_new.transpose(0, 2, 1), lf_new[:, :, None])
    return o_t.transpose(0, 2, 1).reshape(n, nh * hd)


def kernel(x_prompt, x_sample, c_prompt, c_sample, state_gla, cache_k, cache_v, cache_logf, page_table,
           ada_w, ada_b, gla_w_in, gla_w_g2, gla_b_g, gla_onorm, gla_w_out, kv_ada_w, kv_ada_b, w_kv, b_f,
           k_norm, fox_w_in, q_norm, fox_w_out):
    bp, seq, d = x_prompt.shape
    ns = x_sample.shape[0]
    n_a = gla_w_in.shape[0]
    depth = ada_w.shape[0]
    nh = b_f.shape[0]
    d_b = nh * FOX_HD
    dk, dv = state_gla.shape[-2:]

    pad = (-(ns + bp)) % 8
    c_all = jnp.concatenate([c_sample, c_prompt, jnp.zeros((pad, d), F32)], axis=0)
    mod = _ada(c_all, ada_w, ada_b)
    kv_mod = _ada(c_all, kv_ada_w[None], kv_ada_b[None])[0]

    def split_mod(m, n_parts):
        parts = jnp.split(m, n_parts, axis=-1)
        return ([p[ns:ns + bp, None, :] for p in parts],
                [p[None, :ns, :] for p in parts])

    xp = x_prompt
    xs = x_sample.reshape(1, ns, d)
    sp_states, ss_states = [], None
    for layer in range(n_a):
        mod_p, mod_s = split_mod(mod[layer], 3)
        q, k, v, g, la = _gla_in(xp, mod_p[0], mod_p[1], gla_w_in[layer], gla_w_g2[layer], gla_b_g[layer])
        o, s_fin = _gla_scan(q, k, v, la)
        sp_states.append(s_fin)
        xp = _out_proj(o, g, xp, mod_p[2], gla_w_out[layer], gla_onorm[layer])
        q, k, v, g, la = _gla_in(xs, mod_s[0], mod_s[1], gla_w_in[layer], gla_w_g2[layer], gla_b_g[layer])
        o, ss_states = _gla_step(q[0], k[0], la[0], v[0], state_gla, layer, ss_states)
        xs = _out_proj(o.reshape(1, ns, -1), g, xs, mod_s[2], gla_w_out[layer], gla_onorm[layer])

    kvm_p, kvm_s = split_mod(kv_mod, 2)
    kt_p, vt_p, lft_p, lf_p, kb_p, vtb_p = _kv_proj(xp, kvm_p[0], kvm_p[1], w_kv, b_f, k_norm)
    kt_s, vt_s, lft_s, _, _, _ = _kv_proj(xs, kvm_s[0], kvm_s[1], w_kv, b_f, k_norm)
    f2_p = _cumsum_rows(lf_p)
    ka_p = _key_aug(kb_p, f2_p)
    fq_p = f2_p.reshape(bp, seq, nh // 2, 2).transpose(0, 2, 3, 1)
    k_new = kt_s[0].transpose(2, 0, 1)
    v_new = vt_s[0].transpose(2, 0, 1)
    lf_new = lft_s[0].T

    for j in range(depth - n_a):
        mod_p, mod_s = split_mod(mod[n_a + j], 3)
        q, g = _fox_in(xp, mod_p[0], mod_p[1], fox_w_in[j], q_norm[j])
        o = _fox_attn(q, ka_p, vtb_p, fq_p)
        xp = _out_proj(o, g, xp, mod_p[2], fox_w_out[j])
        q, g = _fox_in(xs, mod_s[0], mod_s[1], fox_w_in[j], q_norm[j])
        o = _decode_attn(q.reshape(ns, nh, FOX_HD), k_new, v_new, lf_new, cache_k, cache_v, cache_logf, page_table)
        xs = _out_proj(o.reshape(1, ns, d_b), g, xs, mod_s[2], fox_w_out[j])

    return (xp, xs.reshape(ns, 1, d),
            jnp.stack(sp_states), ss_states,
            kt_p.transpose(0, 3, 1, 2), vt_p.transpose(0, 3, 1, 2), lft_p.transpose(0, 2, 1),
            k_new[:, None], v_new[:, None], lf_new[:, None])
```

```python
import functools

import jax
import jax.numpy as jnp
from jax import lax
from jax.experimental import pallas as pl
from jax.experimental.pallas import tpu as pltpu

F32 = jnp.float32
BF16 = jnp.bfloat16

EPS = 1e-6
GATE_TAU = 16.0
GLA_HEADS = 4
GLA_CHUNK = 64
GLA_SUB = 16
FOX_HD = 64
LANES = 128
NEG_BIG = -1e30
LOG2E = 1.4426950408889634
VMEM_LIMIT = 56 * 1024 * 1024

ROW_TILE = 512
ATTN_TK = 512
SCAN_T = 512
GLA_STEP_G = 4


def _params(sem):
    return pltpu.CompilerParams(dimension_semantics=sem, vmem_limit_bytes=VMEM_LIMIT)


def _silu(x):
    return x / (1.0 + jnp.exp(-x))


def _log_sigmoid(x):
    return jnp.minimum(x, 0.0) - jnp.log1p(jnp.exp(-jnp.abs(x)))


def _rms(x):
    return x * lax.rsqrt(jnp.mean(x * x, axis=-1, keepdims=True) + EPS)


def _dot(a, b):
    return jnp.dot(a, b, preferred_element_type=F32)


def _dot_nt(a, b):
    return lax.dot_general(a, b, (((1,), (1,)), ((), ())), preferred_element_type=F32)


def _dot_tn(a, b):
    return lax.dot_general(a, b, (((0,), (0,)), ((), ())), preferred_element_type=F32)


def _split3(x):
    p0 = x.astype(BF16)
    r1 = x - p0.astype(F32)
    p1 = r1.astype(BF16)
    p2 = (r1 - p1.astype(F32)).astype(BF16)
    return p0, p1, p2


def _exact_left_mul(mat01, x):
    p0, p1, p2 = _split3(x)
    return _dot(mat01, p0) + _dot(mat01, p1) + _dot(mat01, p2)


def _modulated_norm(x_ref, shift_ref, scale_ref):
    h = _rms(x_ref[...]) * (1.0 + scale_ref[...]) + shift_ref[...]
    return h.astype(BF16)


def _head_rms64(p):
    lane = lax.broadcasted_iota(jnp.int32, (1, LANES), 1)
    low = lane < FOX_HD
    outs = []
    for c in range(p.shape[-1] // LANES):
        blk = p[:, c * LANES:(c + 1) * LANES]
        sq = blk * blk
        s_lo = jnp.sum(jnp.where(low, sq, 0.0), axis=-1, keepdims=True)
        s_hi = jnp.sum(jnp.where(low, 0.0, sq), axis=-1, keepdims=True)
        r_lo = lax.rsqrt(s_lo * (1.0 / FOX_HD) + EPS)
        r_hi = lax.rsqrt(s_hi * (1.0 / FOX_HD) + EPS)
        outs.append(blk * jnp.where(low, r_lo, r_hi))
    return jnp.concatenate(outs, axis=-1)


def _ada_kernel(c_ref, w_ref, b_ref, o_ref):
    a = _silu(c_ref[...]).astype(BF16)
    o_ref[...] = _dot(a, w_ref[...].astype(BF16)) + b_ref[...]


def _ada(c_all, w, b, tn=1024):
    nl, d, n = w.shape
    rows = c_all.shape[0]
    return pl.pallas_call(
        _ada_kernel,
        grid=(nl, n // tn),
        in_specs=[
            pl.BlockSpec((rows, d), lambda l, j: (0, 0)),
            pl.BlockSpec((None, d, tn), lambda l, j: (l, 0, j)),
            pl.BlockSpec((None, 1, tn), lambda l, j: (l, 0, j)),
        ],
        out_specs=pl.BlockSpec((None, rows, tn), lambda l, j: (l, 0, j)),
        out_shape=jax.ShapeDtypeStruct((nl, rows, n), F32),
        compiler_params=_params(("parallel", "parallel")),
        name="ada_mod",
    )(c_all, w, b.reshape(nl, 1, n))


def _row_spec(tm, width):
    return pl.BlockSpec((None, tm, width), lambda b, i: (b, i, 0))


def _mod_spec(mod, tm):
    if mod.shape[1] == 1:
        return pl.BlockSpec((None, 1, mod.shape[2]), lambda b, i: (b, 0, 0))
    return pl.BlockSpec((None, tm, mod.shape[2]), lambda b, i: (b, i, 0))


def _full_spec(a):
    nd = a.ndim
    return pl.BlockSpec(a.shape, lambda b, i: (0,) * nd)


def _row_tile(length):
    return min(ROW_TILE, length)


def _gla_in_kernel(x_ref, sh_ref, sc_ref, wq_ref, wk_ref, wv_ref, wg_ref, wz_ref, wg2_ref, bg_ref,
                   q_ref, k_ref, v_ref, g_ref, la_ref, *, q_scale):
    h = _modulated_norm(x_ref, sh_ref, sc_ref)
    q_ref[...] = _dot(h, wq_ref[...]) * q_scale
    k_ref[...] = _dot(h, wk_ref[...])
    v_ref[...] = _dot(h, wv_ref[...])
    g_ref[...] = _dot(h, wg_ref[...])
    z = _dot(h, wz_ref[...])
    t = _dot(z.astype(BF16), wg2_ref[...]) + bg_ref[...]
    la_ref[...] = _log_sigmoid(t) * (1.0 / GATE_TAU)


def _gla_in(x, shift, scale, w_in, w_g2, b_g):
    bsz, length, d = x.shape
    dk_tot = w_g2.shape[1]
    rank = w_g2.shape[0]
    dv_tot = (w_in.shape[1] - 2 * dk_tot - rank) // 2
    dk = dk_tot // GLA_HEADS
    wb = w_in.astype(BF16)
    wq = wb[:, :dk_tot]
    wk = wb[:, dk_tot:2 * dk_tot]
    wv = wb[:, 2 * dk_tot:2 * dk_tot + dv_tot]
    wg = wb[:, 2 * dk_tot + dv_tot:2 * dk_tot + 2 * dv_tot]
    wz = wb[:, 2 * dk_tot + 2 * dv_tot:]
    wg2 = w_g2.astype(BF16)
    bg = b_g.reshape(1, dk_tot)
    tm = _row_tile(length)
    weights = (wq, wk, wv, wg, wz, wg2, bg)
    widths = (dk_tot, dk_tot, dv_tot, dv_tot, dk_tot)
    return pl.pallas_call(
        functools.partial(_gla_in_kernel, q_scale=dk ** -0.5),
        grid=(bsz, length // tm),
        in_specs=[_row_spec(tm, d), _mod_spec(shift, tm), _mod_spec(scale, tm)]
        + [_full_spec(w) for w in weights],
        out_specs=[_row_spec(tm, w) for w in widths],
        out_shape=[jax.ShapeDtypeStruct((bsz, length, w), F32) for w in widths],
        compiler_params=_params(("parallel", "parallel")),
        name="gla_in_proj",
    )(x, shift, scale, *weights)


def _gla_chunk_head(q, k, v, b, s_old, rows):
    c, dk = q.shape
    nsub = c // GLA_SUB
    b_last = b[c - 1:c, :]
    o = _dot((q * jnp.exp(b)).astype(BF16), s_old.astype(BF16))
    kd = k * jnp.exp(b_last - b)
    upd = _dot_tn(kd.astype(BF16), v.astype(BF16))
    decay_col = jnp.exp(jnp.transpose(jnp.broadcast_to(b_last, (dk, dk))))
    s_new = jnp.concatenate([decay_col] * (s_old.shape[1] // dk), axis=1) * s_old + upd
    qcat, kcat = [], []
    for j in range(nsub - 1):
        r_j = b[(j + 1) * GLA_SUB - 1:(j + 1) * GLA_SUB, :]
        in_j = (rows >= j * GLA_SUB) & (rows < (j + 1) * GLA_SUB)
        after_j = rows >= (j + 1) * GLA_SUB
        kcat.append(jnp.where(in_j, k * jnp.exp(jnp.where(in_j, r_j - b, 0.0)), 0.0))
        qcat.append(jnp.where(after_j, q * jnp.exp(jnp.where(after_j, b - r_j, 0.0)), 0.0))
    att = _dot_nt(jnp.concatenate(qcat, axis=1).astype(BF16), jnp.concatenate(kcat, axis=1).astype(BF16))
    sub_row = lax.broadcasted_iota(jnp.int32, (GLA_SUB, 1), 0)
    col_id = lax.broadcasted_iota(jnp.int32, (1, c), 1)
    diag_blocks = []
    for j in range(nsub):
        lo = j * GLA_SUB
        qj = q[lo:lo + GLA_SUB, :]
        bj = b[lo:lo + GLA_SUB, :]
        blk = jnp.zeros((GLA_SUB, c), F32)
        for t in range(GLA_SUB):
            src = lo + t
            valid = sub_row >= t
            e = jnp.exp(jnp.where(valid, bj - b[src:src + 1, :], 0.0))
            col = jnp.sum(qj * k[src:src + 1, :] * e, axis=-1, keepdims=True)
            blk = jnp.where((col_id == src) & valid, col, blk)
        diag_blocks.append(blk)
    att = att + jnp.concatenate(diag_blocks, axis=0)
    o = o + _dot(att.astype(BF16), v.astype(BF16))
    return o, s_new


def _gla_scan_kernel(q_ref, k_ref, v_ref, la_ref, o_ref, sfin_ref, s_sc, *, n_chunks):
    t = pl.program_id(1)
    dk = q_ref.shape[-1] // GLA_HEADS
    dv = v_ref.shape[-1] // GLA_HEADS
    c = GLA_CHUNK

    @pl.when(t == 0)
    def _():
        s_sc[...] = jnp.zeros_like(s_sc)

    tri = (lax.broadcasted_iota(jnp.int32, (c, c), 0) >= lax.broadcasted_iota(jnp.int32, (c, c), 1))
    tri = jnp.where(tri, 1.0, 0.0).astype(BF16)
    rows = lax.broadcasted_iota(jnp.int32, (c, 1), 0)

    def chunk(ci, carry):
        r0 = pl.multiple_of(ci * c, c)
        b_all = _exact_left_mul(tri, la_ref[pl.ds(r0, c), :])
        for h in range(GLA_HEADS):
            o, s_new = _gla_chunk_head(
                q_ref[pl.ds(r0, c), h * dk:(h + 1) * dk],
                k_ref[pl.ds(r0, c), h * dk:(h + 1) * dk],
                v_ref[pl.ds(r0, c), h * dv:(h + 1) * dv],
                b_all[:, h * dk:(h + 1) * dk],
                s_sc[h], rows)
            o_ref[pl.ds(r0, c), h * dv:(h + 1) * dv] = o
            s_sc[h] = s_new
        return carry

    lax.fori_loop(0, n_chunks, chunk, 0)

    @pl.when(t == pl.num_programs(1) - 1)
    def _():
        sfin_ref[...] = s_sc[...]


def _gla_scan(q, k, v, la):
    bsz, length, dk_tot = q.shape
    dv_tot = v.shape[-1]
    dk, dv = dk_tot // GLA_HEADS, dv_tot // GLA_HEADS
    tt = min(SCAN_T, length)
    return pl.pallas_call(
        functools.partial(_gla_scan_kernel, n_chunks=tt // GLA_CHUNK),
        grid=(bsz, length // tt),
        in_specs=[_row_spec(tt, dk_tot), _row_spec(tt, dk_tot), _row_spec(tt, dv_tot), _row_spec(tt, dk_tot)],
        out_specs=[_row_spec(tt, dv_tot),
                   pl.BlockSpec((None, GLA_HEADS, dk, dv), lambda b, i: (b, 0, 0, 0))],
        out_shape=[jax.ShapeDtypeStruct((bsz, length, dv_tot), F32),
                   jax.ShapeDtypeStruct((bsz, GLA_HEADS, dk, dv), F32)],
        scratch_shapes=[pltpu.VMEM((GLA_HEADS, dk, dv), F32)],
        compiler_params=_params(("parallel", "arbitrary")),
        name="gla_scan",
    )(q, k, v, la)


def _gla_step_kernel(qt_ref, kt_ref, lat_ref, v_ref, s_ref, *rest, n_prev):
    if n_prev:
        prev_ref, o_ref, so_ref = rest
        so_ref[0:n_prev] = prev_ref[...]
    else:
        o_ref, so_ref = rest
    dk, dv = s_ref.shape[-2:]
    for g in range(s_ref.shape[0]):
        for h in range(GLA_HEADS):
            q_col = qt_ref[h * dk:(h + 1) * dk, g:g + 1]
            k_col = kt_ref[h * dk:(h + 1) * dk, g:g + 1]
            a_col = jnp.exp(lat_ref[h * dk:(h + 1) * dk, g:g + 1])
            v_row = v_ref[g:g + 1, h * dv:(h + 1) * dv]
            s_new = a_col * s_ref[g, h] + k_col * v_row
            so_ref[n_prev, g, h] = s_new
            o_ref[g:g + 1, h * dv:(h + 1) * dv] = jnp.sum(q_col * s_new, axis=0, keepdims=True)


def _gla_step(q, k, la, v, states, layer, prev=None):
    n, dk_tot = q.shape
    dv_tot = v.shape[-1]
    dk, dv = states.shape[-2:]
    n_prev = 0 if prev is None else prev.shape[0]
    g = GLA_STEP_G
    steps = n // g

    def cols(a):
        return a.reshape(steps, g, a.shape[-1]).transpose(0, 2, 1)

    col_spec = pl.BlockSpec((None, dk_tot, g), lambda i: (i, 0, 0))
    prev_in = [] if prev is None else [prev]
    prev_spec = [] if prev is None else [pl.BlockSpec((n_prev, g, GLA_HEADS, dk, dv), lambda i: (0, i, 0, 0, 0))]
    o, stacked = pl.pallas_call(
        functools.partial(_gla_step_kernel, n_prev=n_prev),
        grid=(steps,),
        in_specs=[col_spec, col_spec, col_spec,
                  pl.BlockSpec((None, g, dv_tot), lambda i: (i, 0, 0)),
                  pl.BlockSpec((None, g, GLA_HEADS, dk, dv), lambda i: (layer, i, 0, 0, 0))] + prev_spec,
        out_specs=[pl.BlockSpec((None, g, dv_tot), lambda i: (i, 0, 0)),
                   pl.BlockSpec((n_prev + 1, g, GLA_HEADS, dk, dv), lambda i: (0, i, 0, 0, 0))],
        out_shape=[jax.ShapeDtypeStruct((steps, g, dv_tot), F32),
                   jax.ShapeDtypeStruct((n_prev + 1,) + states.shape[1:], F32)],
        compiler_params=_params(("parallel",)),
        name="gla_step",
    )(cols(q), cols(k), cols(la), v.reshape(steps, g, dv_tot), states, *prev_in)
    return o.reshape(n, dv_tot), stacked


def _out_kernel(o_ref, g_ref, x_ref, gate_ref, w_ref, *rest, head_dim):
    if head_dim:
        onorm_ref, y_ref = rest
        o = o_ref[...]
        o = jnp.concatenate(
            [_rms(o[:, h * head_dim:(h + 1) * head_dim]) for h in range(o.shape[-1] // head_dim)], axis=-1)
        o = o * onorm_ref[...]
    else:
        (y_ref,) = rest
        o = o_ref[...]
    a = (o * _silu(g_ref[...])).astype(BF16)
    y_ref[...] = x_ref[...] + gate_ref[...] * _dot(a, w_ref[...])


def _out_proj(o, g, x, gate, w_out, onorm=None):
    bsz, length, d = x.shape
    width = o.shape[-1]
    tm = _row_tile(length)
    wb = w_out.astype(BF16)
    extra, head_dim = [], 0
    if onorm is not None:
        head_dim = onorm.shape[0]
        extra = [jnp.tile(onorm, width // head_dim).reshape(1, width)]
    return pl.pallas_call(
        functools.partial(_out_kernel, head_dim=head_dim),
        grid=(bsz, length // tm),
        in_specs=[_row_spec(tm, width), _row_spec(tm, width), _row_spec(tm, d), _mod_spec(gate, tm),
                  _full_spec(wb)] + [_full_spec(e) for e in extra],
        out_specs=_row_spec(tm, d),
        out_shape=jax.ShapeDtypeStruct((bsz, length, d), F32),
        compiler_params=_params(("parallel", "parallel")),
        name="out_proj",
    )(o, g, x, gate, wb, *extra)


def _kv_kernel(x_ref, sh_ref, sc_ref, wk_ref, wv_ref, wf_ref, wft_ref, bf_ref, bft_ref, kn_ref,
               kt_ref, vt_ref, lft_ref, lf_ref, kb_ref, vtb_ref):
    h = _modulated_norm(x_ref, sh_ref, sc_ref)
    k = _head_rms64(_dot(h, wk_ref[...])) * kn_ref[...]
    v = _dot(h, wv_ref[...])
    k_t = jnp.transpose(k)
    v_t = jnp.transpose(v)
    for hd in range(kt_ref.shape[0]):
        kt_ref[hd] = k_t[hd * FOX_HD:(hd + 1) * FOX_HD, :]
        vt_ref[hd] = v_t[hd * FOX_HD:(hd + 1) * FOX_HD, :]
    kb_ref[...] = k.astype(BF16)
    vtb_ref[...] = v_t.astype(BF16)
    lf_ref[...] = _log_sigmoid(_dot(h, wf_ref[...]) + bf_ref[...])
    lft_ref[...] = _log_sigmoid(_dot_nt(wft_ref[...], h) + bft_ref[...])


def _kv_proj(x, shift, scale, w_kv, b_f, k_norm):
    bsz, length, d = x.shape
    nh = b_f.shape[0]
    d_b = (w_kv.shape[1] - nh) // 2
    hd = d_b // nh
    wb = w_kv.astype(BF16)
    weights = (wb[:, :d_b], wb[:, d_b:2 * d_b], wb[:, 2 * d_b:], wb[:, 2 * d_b:].T,
               b_f.reshape(1, nh), b_f.reshape(nh, 1), jnp.tile(k_norm, nh).reshape(1, d_b))
    tm = _row_tile(length)
    tok_minor = pl.BlockSpec((None, nh, hd, tm), lambda b, i: (b, 0, 0, i))
    return pl.pallas_call(
        _kv_kernel,
        grid=(bsz, length // tm),
        in_specs=[_row_spec(tm, d), _mod_spec(shift, tm), _mod_spec(scale, tm)]
        + [_full_spec(w) for w in weights],
        out_specs=[tok_minor, tok_minor,
                   pl.BlockSpec((None, nh, tm), lambda b, i: (b, 0, i)),
                   _row_spec(tm, nh), _row_spec(tm, d_b),
                   pl.BlockSpec((None, d_b, tm), lambda b, i: (b, 0, i))],
        out_shape=[jax.ShapeDtypeStruct((bsz, nh, hd, length), F32),
                   jax.ShapeDtypeStruct((bsz, nh, hd, length), F32),
                   jax.ShapeDtypeStruct((bsz, nh, length), F32),
                   jax.ShapeDtypeStruct((bsz, length, nh), F32),
                   jax.ShapeDtypeStruct((bsz, length, d_b), BF16),
                   jax.ShapeDtypeStruct((bsz, d_b, length), BF16)],
        compiler_params=_params(("parallel", "parallel")),
        name="kv_proj",
    )(x, shift, scale, *weights)


def _fox_in_kernel(x_ref, sh_ref, sc_ref, wq_ref, wg_ref, qn_ref, q_ref, g_ref):
    h = _modulated_norm(x_ref, sh_ref, sc_ref)
    q_ref[...] = (_head_rms64(_dot(h, wq_ref[...])) * qn_ref[...]).astype(BF16)
    g_ref[...] = _dot(h, wg_ref[...])


def _fox_in(x, shift, scale, w_in, q_norm):
    bsz, length, d = x.shape
    d_b = w_in.shape[1] // 2
    wb = w_in.astype(BF16)
    qn = (jnp.tile(q_norm, d_b // FOX_HD) * (FOX_HD ** -0.5 * LOG2E)).reshape(1, d_b)
    weights = (wb[:, :d_b], wb[:, d_b:], qn)
    tm = _row_tile(length)
    return pl.pallas_call(
        _fox_in_kernel,
        grid=(bsz, length // tm),
        in_specs=[_row_spec(tm, d), _mod_spec(shift, tm), _mod_spec(scale, tm)]
        + [_full_spec(w) for w in weights],
        out_specs=[_row_spec(tm, d_b), _row_spec(tm, d_b)],
        out_shape=[jax.ShapeDtypeStruct((bsz, length, d_b), BF16),
                   jax.ShapeDtypeStruct((bsz, length, d_b), F32)],
        compiler_params=_params(("parallel", "parallel")),
        name="fox_in_proj",
    )(x, shift, scale, *weights)


def _cumsum_kernel(lf_ref, f_ref, carry_sc):
    r = lf_ref.shape[0]

    @pl.when(pl.program_id(1) == 0)
    def _():
        carry_sc[...] = jnp.zeros_like(carry_sc)

    tri = lax.broadcasted_iota(jnp.int32, (r, r), 0) >= lax.broadcasted_iota(jnp.int32, (r, r), 1)
    tri = jnp.where(tri, 1.0, 0.0).astype(BF16)
    f = _exact_left_mul(tri, lf_ref[...]) + carry_sc[...]
    f_ref[...] = f * LOG2E
    carry_sc[...] = f[r - 1:r, :]


def _cumsum_rows(lf):
    bsz, length, nh = lf.shape
    r = min(ROW_TILE, length)
    return pl.pallas_call(
        _cumsum_kernel,
        grid=(bsz, length // r),
        in_specs=[_row_spec(r, nh)],
        out_specs=_row_spec(r, nh),
        out_shape=jax.ShapeDtypeStruct(lf.shape, F32),
        scratch_shapes=[pltpu.VMEM((1, nh), F32)],
        compiler_params=_params(("parallel", "arbitrary")),
        name="forget_cumsum",
    )(lf)


N_PIECES = 3


def _key_aug_kernel(kb_ref, f_ref, place_k_ref, place_f_ref, ones_ref, ka_ref):
    bias = ones_ref[...]
    for j, piece in enumerate(_split3(f_ref[...])):
        bias = bias - _dot(piece, place_f_ref[j])
    for p in range(ka_ref.shape[0] // 2):
        pair = _dot(kb_ref[:, p * LANES:(p + 1) * LANES], place_k_ref[...]) + bias[:, 2 * p * LANES:2 * (p + 1) * LANES]
        ka_ref[2 * p] = pair[:, :LANES].astype(BF16)
        ka_ref[2 * p + 1] = pair[:, LANES:].astype(BF16)


def _key_aug(kb, f2):
    bsz, length, d_b = kb.shape
    nh = f2.shape[-1]
    tm = _row_tile(length)
    col = jnp.arange(nh * LANES)
    row = jnp.arange(LANES)
    place_k = ((col[None, :2 * LANES] // LANES == row[:, None] // FOX_HD)
               & (col[None, :2 * LANES] % LANES == row[:, None] % FOX_HD)).astype(BF16)
    head = jnp.arange(nh)
    place_f = jnp.stack([(col[None, :] == head[:, None] * LANES + FOX_HD + j) for j in range(N_PIECES)]
                        ).astype(BF16)
    ones = ((col % LANES >= FOX_HD + N_PIECES) & (col % LANES < FOX_HD + 2 * N_PIECES)).astype(F32)[None, :]
    return pl.pallas_call(
        _key_aug_kernel,
        grid=(bsz, length // tm),
        in_specs=[_row_spec(tm, d_b), _row_spec(tm, nh), _full_spec(place_k), _full_spec(place_f),
                  _full_spec(ones)],
        out_specs=pl.BlockSpec((None, nh, tm, LANES), lambda b, i: (b, 0, i, 0)),
        out_shape=jax.ShapeDtypeStruct((bsz, nh, length, LANES), BF16),
        compiler_params=_params(("parallel", "parallel")),
        name="fox_key_aug",
    )(kb, f2, place_k, place_f, ones)


def _fox_attn_kernel(pt_ref, q_ref, ka_ref, vt_ref, fq_ref, *rest, n_pages, steps_per_sample):
    del pt_ref
    dec_refs = rest[:3 * n_pages + 5]
    o_ref, od_ref, qt_sc, m_sc, l_sc, acc_sc, s_sc = rest[3 * n_pages + 5:]
    step = (pl.program_id(0) * pl.num_programs(1) + pl.program_id(1)) * pl.num_programs(2) + pl.program_id(2)

    @pl.when(step % steps_per_sample == 0)
    def _():
        _decode_body(dec_refs, od_ref, s_sc, n_pages)

    qi = pl.program_id(2)
    tq = q_ref.shape[0]
    tk = tq
    q_t = jnp.transpose(q_ref[...].astype(F32))
    brow = lax.broadcasted_iota(jnp.int32, (FOX_HD, 1), 0)
    ones_rows = jnp.broadcast_to(jnp.where(brow < N_PIECES, 1.0, 0.0), (FOX_HD, tq))

    def q_aug(i, c):
        extra = ones_rows
        if c is not None:
            for j, piece in enumerate(_split3(c)):
                extra = jnp.where(brow == N_PIECES + j, piece.astype(F32), extra)
        return jnp.concatenate([q_t[i * FOX_HD:(i + 1) * FOX_HD, :], extra], axis=0).astype(BF16)

    q_plain = (q_aug(0, None), q_aug(1, None))

    def reset():
        m_sc[...] = jnp.full_like(m_sc, NEG_BIG)
        l_sc[...] = jnp.zeros_like(l_sc)
        acc_sc[...] = jnp.zeros_like(acc_sc)

    def online_block(kj, causal):
        c0 = pl.multiple_of(kj * tk, tk)
        ts = []
        for i in range(2):
            t = _dot(ka_ref[i, pl.ds(c0, tk), :], q_plain[i])
            if causal:
                key = lax.broadcasted_iota(jnp.int32, (tk, tq), 0)
                qry = lax.broadcasted_iota(jnp.int32, (tk, tq), 1)
                t = jnp.where(key <= qry, t, NEG_BIG)
            ts.append(t)
        ps, alphas = [], []
        for i in range(2):
            fq = fq_ref[i:i + 1, :]
            m_old = m_sc[i]
            m_new = jnp.maximum(m_old, jnp.max(ts[i], axis=0, keepdims=True) + fq)
            alpha = jnp.exp2(m_old - m_new)
            p = jnp.exp2(ts[i] + (fq - m_new))
            l_sc[i] = alpha * l_sc[i] + jnp.sum(p, axis=0, keepdims=True)
            m_sc[i] = m_new
            ps.append(p.astype(BF16))
            alphas.append(alpha)
        for i in range(2):
            acc_sc[i] = alphas[i] * acc_sc[i] + _dot(vt_ref[i * FOX_HD:(i + 1) * FOX_HD, pl.ds(c0, tk)], ps[i])

    def result():
        return jnp.concatenate([acc_sc[0] / l_sc[0], acc_sc[1] / l_sc[1]], axis=0)

    l_sc[...] = jnp.zeros_like(l_sc)
    acc_sc[...] = jnp.zeros_like(acc_sc)
    q0 = pl.multiple_of(qi * tk, tk)
    for i in range(2):
        k_t = jnp.transpose(ka_ref[i, pl.ds(q0, tk), :].astype(F32))
        self_logit = jnp.sum(q_t[i * FOX_HD:(i + 1) * FOX_HD, :] * k_t[:FOX_HD, :], axis=0, keepdims=True)
        qt_sc[i] = q_aug(i, fq_ref[i:i + 1, :] - self_logit)

    def fixed_shift_blocks(kjs, causal=False):
        ps, sums = [], [0.0, 0.0]
        for kj in kjs:
            c0 = pl.multiple_of(kj * tk, tk)
            for i in range(2):
                x = _dot(ka_ref[i, pl.ds(c0, tk), :], qt_sc[i])
                if causal:
                    key = lax.broadcasted_iota(jnp.int32, (tk, tq), 0)
                    qry = lax.broadcasted_iota(jnp.int32, (tk, tq), 1)
                    x = jnp.where(key <= qry, x, NEG_BIG)
                p = jnp.exp2(x)
                sums[i] = sums[i] + jnp.sum(p, axis=0, keepdims=True)
                ps.append(p.astype(BF16))
        for i in range(2):
            l_sc[i] = l_sc[i] + sums[i]
            pv = 0.0
            for n, kj in enumerate(kjs):
                c0 = pl.multiple_of(kj * tk, tk)
                pv = pv + _dot(vt_ref[i * FOX_HD:(i + 1) * FOX_HD, pl.ds(c0, tk)], ps[2 * n + i])
            acc_sc[i] = acc_sc[i] + pv

    def quad_body(j, carry):
        fixed_shift_blocks(tuple(4 * j + n for n in range(4)))
        return carry

    lax.fori_loop(0, qi // 4, quad_body, 0)
    done = (qi // 4) * 4

    @pl.when(qi % 4 >= 2)
    def _():
        fixed_shift_blocks((done, done + 1))

    @pl.when(qi % 2 == 1)
    def _():
        fixed_shift_blocks((qi - 1,))

    fixed_shift_blocks((qi,), causal=True)
    o_t = result()
    bad = (jnp.sum(jnp.where(jnp.isfinite(o_t), 0.0, 1.0))
           + jnp.sum(jnp.where(jnp.isfinite(l_sc[...]), 0.0, 1.0)))

    @pl.when(bad == 0.0)
    def _():
        o_ref[...] = jnp.transpose(o_t)

    @pl.when(bad != 0.0)
    def _():
        reset()

        def body(kj, carry):
            online_block(kj, False)
            return carry

        lax.fori_loop(0, qi, body, 0)
        online_block(qi, True)
        o_ref[...] = jnp.transpose(result())


def _fox_attn(q, ka, vt, fq, q_s, k_new, v_new, lf_new, cache_k, cache_v, cache_logf, page_table):
    bsz, length, d_b = q.shape
    pairs = fq.shape[1]
    tq = min(ATTN_TK, length)
    nq = length // tq
    n, n_pages = page_table.shape
    _, ps, nh, hd = cache_k.shape
    steps_per_sample, rem = divmod(bsz * pairs * nq, n)
    assert rem == 0 and steps_per_sample >= 1, "prompt grid must be a multiple of the sample count"

    def sample_of(b, p, i):
        return ((b * pairs + p) * nq + i) // steps_per_sample

    dec_operands, dec_specs, dec_out_spec = _decode_operands(
        q_s, k_new, v_new, lf_new, cache_k, cache_v, cache_logf, page_table, sample_of)
    resident = pl.Buffered(1)
    grid_spec = pltpu.PrefetchScalarGridSpec(
        num_scalar_prefetch=1,
        grid=(bsz, pairs, nq),
        in_specs=[
            pl.BlockSpec((None, tq, LANES), lambda b, p, i, pt: (b, i, p)),
            pl.BlockSpec((None, 2, length, LANES), lambda b, p, i, pt: (b, p, 0, 0), pipeline_mode=resident),
            pl.BlockSpec((None, LANES, length), lambda b, p, i, pt: (b, p, 0), pipeline_mode=resident),
            pl.BlockSpec((None, None, 2, tq), lambda b, p, i, pt: (b, p, 0, i)),
        ] + dec_specs,
        out_specs=[pl.BlockSpec((None, tq, LANES), lambda b, p, i, pt: (b, i, p)), dec_out_spec],
        scratch_shapes=[pltpu.VMEM((2, LANES, tq), BF16),
                        pltpu.VMEM((2, 1, tq), F32), pltpu.VMEM((2, 1, tq), F32),
                        pltpu.VMEM((2, FOX_HD, tq), F32),
                        pltpu.VMEM((nh, n_pages * ps), F32)],
    )
    o_p, o_t = pl.pallas_call(
        functools.partial(_fox_attn_kernel, n_pages=n_pages, steps_per_sample=steps_per_sample),
        grid_spec=grid_spec,
        out_shape=[jax.ShapeDtypeStruct((bsz, length, d_b), F32), jax.ShapeDtypeStruct((n, hd, nh), F32)],
        compiler_params=_params(("arbitrary", "arbitrary", "arbitrary")),
        name="fox_attn",
    )(page_table.reshape(-1), q, ka, vt, fq, *dec_operands)
    return o_p, o_t.transpose(0, 2, 1).reshape(n, nh * hd)


def _decode_body(refs, o_ref, s_sc, n_pages):
    k_pages = refs[:n_pages]
    v_pages = refs[n_pages:2 * n_pages]
    f_pages = refs[2 * n_pages:3 * n_pages]
    qt_ref, q_ref, kn_ref, vnt_ref, lfn_ref = refs[3 * n_pages:]
    nh, hd, ps = k_pages[0].shape

    later = lax.broadcasted_iota(jnp.int32, (ps, ps), 0) > lax.broadcasted_iota(jnp.int32, (ps, ps), 1)
    later = jnp.where(later, 1.0, 0.0).astype(BF16)
    carry = lfn_ref[...]
    for j in reversed(range(n_pages)):
        page = f_pages[j][...]
        within = sum(_dot(piece, later) for piece in _split3(page))
        s_sc[:, j * ps:(j + 1) * ps] = (within + carry) * LOG2E
        carry = carry + jnp.sum(page, axis=-1, keepdims=True)

    for h in range(nh):
        q_col = jnp.broadcast_to(qt_ref[:, h:h + 1].astype(F32), (hd, ps))
        for j in range(n_pages):
            cols = slice(j * ps, (j + 1) * ps)
            s_sc[h:h + 1, cols] = s_sc[h:h + 1, cols] + jnp.sum(k_pages[j][h] * q_col, axis=0, keepdims=True)
    s_self = jnp.sum(q_ref[...].astype(F32) * kn_ref[...], axis=-1, keepdims=True)
    s = s_sc[...]
    m = jnp.maximum(s_self, jnp.max(s, axis=-1, keepdims=True))
    p = jnp.exp2(s - m)
    p_self = jnp.exp2(s_self - m)
    denom = p_self + jnp.sum(p, axis=-1, keepdims=True)
    s_sc[...] = p

    head_lane = lax.broadcasted_iota(jnp.int32, (1, nh), 1)
    o_t = jnp.zeros((hd, nh), F32)
    for h in range(nh):
        acc = jnp.zeros((hd, ps), F32)
        for j in range(n_pages):
            acc = acc + v_pages[j][h] * s_sc[h:h + 1, j * ps:(j + 1) * ps]
        o_t = jnp.where(head_lane == h, jnp.sum(acc, axis=-1, keepdims=True), o_t)

    def as_row(col):
        eye = lax.broadcasted_iota(jnp.int32, (nh, nh), 0) == lax.broadcasted_iota(jnp.int32, (nh, nh), 1)
        return jnp.sum(jnp.where(eye, col, 0.0), axis=0, keepdims=True)

    o_ref[...] = (o_t + vnt_ref[...] * as_row(p_self)) / as_row(denom)


def _decode_operands(q, k_new, v_new, lf_new, cache_k, cache_v, cache_logf, page_table, sample_of):
    n_pages = page_table.shape[1]
    _, ps, nh, hd = cache_k.shape
    k_view = cache_k.transpose(0, 2, 3, 1)
    v_view = cache_v.transpose(0, 2, 3, 1)
    f_view = cache_logf.transpose(0, 2, 1)

    def kv_spec(j):
        return pl.BlockSpec((None, nh, hd, ps), lambda *a: (a[-1][sample_of(*a[:-1]) * n_pages + j], 0, 0, 0))

    def f_spec(j):
        return pl.BlockSpec((None, nh, ps), lambda *a: (a[-1][sample_of(*a[:-1]) * n_pages + j], 0, 0))

    t_spec = pl.BlockSpec((None, hd, nh), lambda *a: (sample_of(*a[:-1]), 0, 0))
    r_spec = pl.BlockSpec((None, nh, hd), lambda *a: (sample_of(*a[:-1]), 0, 0))
    specs = ([kv_spec(j) for j in range(n_pages)] + [kv_spec(j) for j in range(n_pages)]
             + [f_spec(j) for j in range(n_pages)]
             + [t_spec, r_spec, r_spec, t_spec, pl.BlockSpec((None, nh, 1), lambda *a: (sample_of(*a[:-1]), 0, 0))])
    operands = ([k_view] * n_pages + [v_view] * n_pages + [f_view] * n_pages
                + [q.transpose(0, 2, 1), q, k_new, v_new.transpose(0, 2, 1), lf_new[:, :, None]])
    return operands, specs, t_spec


def kernel(x_prompt, x_sample, c_prompt, c_sample, state_gla, cache_k, cache_v, cache_logf, page_table,
           ada_w, ada_b, gla_w_in, gla_w_g2, gla_b_g, gla_onorm, gla_w_out, kv_ada_w, kv_ada_b, w_kv, b_f,
           k_norm, fox_w_in, q_norm, fox_w_out):
    bp, seq, d = x_prompt.shape
    ns = x_sample.shape[0]
    n_a = gla_w_in.shape[0]
    depth = ada_w.shape[0]
    nh = b_f.shape[0]
    d_b = nh * FOX_HD
    dk, dv = state_gla.shape[-2:]

    pad = (-(ns + bp)) % 8
    c_all = jnp.concatenate([c_sample, c_prompt, jnp.zeros((pad, d), F32)], axis=0)
    mod = _ada(c_all, ada_w, ada_b)
    kv_mod = _ada(c_all, kv_ada_w[None], kv_ada_b[None])[0]

    def split_mod(m, n_parts):
        parts = jnp.split(m, n_parts, axis=-1)
        return ([p[ns:ns + bp, None, :] for p in parts],
                [p[None, :ns, :] for p in parts])

    xp = x_prompt
    xs = x_sample.reshape(1, ns, d)
    sp_states, ss_states = [], None
    for layer in range(n_a):
        mod_p, mod_s = split_mod(mod[layer], 3)
        q, k, v, g, la = _gla_in(xp, mod_p[0], mod_p[1], gla_w_in[layer], gla_w_g2[layer], gla_b_g[layer])
        o, s_fin = _gla_scan(q, k, v, la)
        sp_states.append(s_fin)
        xp = _out_proj(o, g, xp, mod_p[2], gla_w_out[layer], gla_onorm[layer])
        q, k, v, g, la = _gla_in(xs, mod_s[0], mod_s[1], gla_w_in[layer], gla_w_g2[layer], gla_b_g[layer])
        o, ss_states = _gla_step(q[0], k[0], la[0], v[0], state_gla, layer, ss_states)
        xs = _out_proj(o.reshape(1, ns, -1), g, xs, mod_s[2], gla_w_out[layer], gla_onorm[layer])

    kvm_p, kvm_s = split_mod(kv_mod, 2)
    kt_p, vt_p, lft_p, lf_p, kb_p, vtb_p = _kv_proj(xp, kvm_p[0], kvm_p[1], w_kv, b_f, k_norm)
    kt_s, vt_s, lft_s, _, _, _ = _kv_proj(xs, kvm_s[0], kvm_s[1], w_kv, b_f, k_norm)
    f2_p = _cumsum_rows(lf_p)
    ka_p = _key_aug(kb_p, f2_p)
    fq_p = f2_p.reshape(bp, seq, nh // 2, 2).transpose(0, 2, 3, 1)
    k_new = kt_s[0].transpose(2, 0, 1)
    v_new = vt_s[0].transpose(2, 0, 1)
    lf_new = lft_s[0].T

    for j in range(depth - n_a):
        mod_p, mod_s = split_mod(mod[n_a + j], 3)
        q_p, g_p = _fox_in(xp, mod_p[0], mod_p[1], fox_w_in[j], q_norm[j])
        q_s, g_s = _fox_in(xs, mod_s[0], mod_s[1], fox_w_in[j], q_norm[j])
        o_p, o_s = _fox_attn(q_p, ka_p, vtb_p, fq_p, q_s.reshape(ns, nh, FOX_HD), k_new, v_new, lf_new,
                             cache_k, cache_v, cache_logf, page_table)
        xp = _out_proj(o_p, g_p, xp, mod_p[2], fox_w_out[j])
        xs = _out_proj(o_s.reshape(1, ns, d_b), g_s, xs, mod_s[2], fox_w_out[j])

    return (xp, xs.reshape(ns, 1, d),
            jnp.stack(sp_states), ss_states,
            kt_p.transpose(0, 3, 1, 2), vt_p.transpose(0, 3, 1, 2), lft_p.transpose(0, 2, 1),
            k_new[:, None], v_new[:, None], lf_new[:, None])
```

```python
import functools

import jax
import jax.numpy as jnp
from jax import lax
from jax.experimental import pallas as pl
from jax.experimental.pallas import tpu as pltpu

F32 = jnp.float32
BF16 = jnp.bfloat16

EPS = 1e-6
GATE_TAU = 16.0
GLA_HEADS = 4
GLA_CHUNK = 64
GLA_SUB = 16
FOX_HD = 64
LANES = 128
NEG_BIG = -1e30
LOG2E = 1.4426950408889634
VMEM_LIMIT = 56 * 1024 * 1024

ROW_TILE = 512
ATTN_TK = 512
SCAN_T = 512
GLA_STEP_G = 4


def _params(sem):
    return pltpu.CompilerParams(dimension_semantics=sem, vmem_limit_bytes=VMEM_LIMIT)


def _silu(x):
    return x / (1.0 + jnp.exp(-x))


def _log_sigmoid(x):
    return jnp.minimum(x, 0.0) - jnp.log1p(jnp.exp(-jnp.abs(x)))


def _rms(x):
    return x * lax.rsqrt(jnp.mean(x * x, axis=-1, keepdims=True) + EPS)


def _dot(a, b):
    return jnp.dot(a, b, preferred_element_type=F32)


def _dot_nt(a, b):
    return lax.dot_general(a, b, (((1,), (1,)), ((), ())), preferred_element_type=F32)


def _dot_tn(a, b):
    return lax.dot_general(a, b, (((0,), (0,)), ((), ())), preferred_element_type=F32)


def _split3(x):
    p0 = x.astype(BF16)
    r1 = x - p0.astype(F32)
    p1 = r1.astype(BF16)
    p2 = (r1 - p1.astype(F32)).astype(BF16)
    return p0, p1, p2


def _exact_left_mul(mat01, x):
    p0, p1, p2 = _split3(x)
    return _dot(mat01, p0) + _dot(mat01, p1) + _dot(mat01, p2)


def _modulated_norm(x_ref, shift_ref, scale_ref):
    h = _rms(x_ref[...]) * (1.0 + scale_ref[...]) + shift_ref[...]
    return h.astype(BF16)


def _head_rms64(p):
    lane = lax.broadcasted_iota(jnp.int32, (1, LANES), 1)
    low = lane < FOX_HD
    outs = []
    for c in range(p.shape[-1] // LANES):
        blk = p[:, c * LANES:(c + 1) * LANES]
        sq = blk * blk
        s_lo = jnp.sum(jnp.where(low, sq, 0.0), axis=-1, keepdims=True)
        s_hi = jnp.sum(jnp.where(low, 0.0, sq), axis=-1, keepdims=True)
        r_lo = lax.rsqrt(s_lo * (1.0 / FOX_HD) + EPS)
        r_hi = lax.rsqrt(s_hi * (1.0 / FOX_HD) + EPS)
        outs.append(blk * jnp.where(low, r_lo, r_hi))
    return jnp.concatenate(outs, axis=-1)


def _ada_kernel(c_ref, w_ref, b_ref, o_ref):
    a = _silu(c_ref[...]).astype(BF16)
    o_ref[...] = _dot(a, w_ref[...].astype(BF16)) + b_ref[...]


def _ada(c_all, w, b, tn=1024):
    nl, d, n = w.shape
    rows = c_all.shape[0]
    return pl.pallas_call(
        _ada_kernel,
        grid=(nl, n // tn),
        in_specs=[
            pl.BlockSpec((rows, d), lambda l, j: (0, 0)),
            pl.BlockSpec((None, d, tn), lambda l, j: (l, 0, j)),
            pl.BlockSpec((None, 1, tn), lambda l, j: (l, 0, j)),
        ],
        out_specs=pl.BlockSpec((None, rows, tn), lambda l, j: (l, 0, j)),
        out_shape=jax.ShapeDtypeStruct((nl, rows, n), F32),
        compiler_params=_params(("parallel", "parallel")),
        name="ada_mod",
    )(c_all, w, b.reshape(nl, 1, n))


def _row_spec(tm, width):
    return pl.BlockSpec((None, tm, width), lambda b, i: (b, i, 0))


def _mod_spec(mod, tm):
    if mod.shape[1] == 1:
        return pl.BlockSpec((None, 1, mod.shape[2]), lambda b, i: (b, 0, 0))
    return pl.BlockSpec((None, tm, mod.shape[2]), lambda b, i: (b, i, 0))


def _full_spec(a):
    nd = a.ndim
    return pl.BlockSpec(a.shape, lambda b, i: (0,) * nd)


def _row_tile(length):
    return min(ROW_TILE, length)


def _gla_in_kernel(x_ref, sh_ref, sc_ref, wq_ref, wk_ref, wv_ref, wg_ref, wz_ref, wg2_ref, bg_ref,
                   q_ref, k_ref, v_ref, g_ref, la_ref, *, q_scale):
    h = _modulated_norm(x_ref, sh_ref, sc_ref)
    q_ref[...] = _dot(h, wq_ref[...]) * q_scale
    k_ref[...] = _dot(h, wk_ref[...])
    v_ref[...] = _dot(h, wv_ref[...])
    g_ref[...] = _dot(h, wg_ref[...])
    z = _dot(h, wz_ref[...])
    t = _dot(z.astype(BF16), wg2_ref[...]) + bg_ref[...]
    la_ref[...] = _log_sigmoid(t) * (1.0 / GATE_TAU)


def _gla_in(x, shift, scale, w_in, w_g2, b_g):
    bsz, length, d = x.shape
    dk_tot = w_g2.shape[1]
    rank = w_g2.shape[0]
    dv_tot = (w_in.shape[1] - 2 * dk_tot - rank) // 2
    dk = dk_tot // GLA_HEADS
    wb = w_in.astype(BF16)
    wq = wb[:, :dk_tot]
    wk = wb[:, dk_tot:2 * dk_tot]
    wv = wb[:, 2 * dk_tot:2 * dk_tot + dv_tot]
    wg = wb[:, 2 * dk_tot + dv_tot:2 * dk_tot + 2 * dv_tot]
    wz = wb[:, 2 * dk_tot + 2 * dv_tot:]
    wg2 = w_g2.astype(BF16)
    bg = b_g.reshape(1, dk_tot)
    tm = _row_tile(length)
    weights = (wq, wk, wv, wg, wz, wg2, bg)
    widths = (dk_tot, dk_tot, dv_tot, dv_tot, dk_tot)
    return pl.pallas_call(
        functools.partial(_gla_in_kernel, q_scale=dk ** -0.5),
        grid=(bsz, length // tm),
        in_specs=[_row_spec(tm, d), _mod_spec(shift, tm), _mod_spec(scale, tm)]
        + [_full_spec(w) for w in weights],
        out_specs=[_row_spec(tm, w) for w in widths],
        out_shape=[jax.ShapeDtypeStruct((bsz, length, w), F32) for w in widths],
        compiler_params=_params(("parallel", "parallel")),
        name="gla_in_proj",
    )(x, shift, scale, *weights)


def _gla_chunk_head(q, k, v, b, s_old, rows):
    c, dk = q.shape
    nsub = c // GLA_SUB
    b_last = b[c - 1:c, :]
    o = _dot((q * jnp.exp(b)).astype(BF16), s_old.astype(BF16))
    kd = k * jnp.exp(b_last - b)
    upd = _dot_tn(kd.astype(BF16), v.astype(BF16))
    decay_col = jnp.exp(jnp.transpose(jnp.broadcast_to(b_last, (dk, dk))))
    s_new = jnp.concatenate([decay_col] * (s_old.shape[1] // dk), axis=1) * s_old + upd
    qcat, kcat = [], []
    for j in range(nsub - 1):
        r_j = b[(j + 1) * GLA_SUB - 1:(j + 1) * GLA_SUB, :]
        in_j = (rows >= j * GLA_SUB) & (rows < (j + 1) * GLA_SUB)
        after_j = rows >= (j + 1) * GLA_SUB
        kcat.append(jnp.where(in_j, k * jnp.exp(jnp.where(in_j, r_j - b, 0.0)), 0.0))
        qcat.append(jnp.where(after_j, q * jnp.exp(jnp.where(after_j, b - r_j, 0.0)), 0.0))
    att = _dot_nt(jnp.concatenate(qcat, axis=1).astype(BF16), jnp.concatenate(kcat, axis=1).astype(BF16))
    sub_row = lax.broadcasted_iota(jnp.int32, (GLA_SUB, 1), 0)
    col_id = lax.broadcasted_iota(jnp.int32, (1, c), 1)
    diag_blocks = []
    for j in range(nsub):
        lo = j * GLA_SUB
        qj = q[lo:lo + GLA_SUB, :]
        bj = b[lo:lo + GLA_SUB, :]
        blk = jnp.zeros((GLA_SUB, c), F32)
        for t in range(GLA_SUB):
            src = lo + t
            valid = sub_row >= t
            e = jnp.exp(jnp.where(valid, bj - b[src:src + 1, :], 0.0))
            col = jnp.sum(qj * k[src:src + 1, :] * e, axis=-1, keepdims=True)
            blk = jnp.where((col_id == src) & valid, col, blk)
        diag_blocks.append(blk)
    att = att + jnp.concatenate(diag_blocks, axis=0)
    o = o + _dot(att.astype(BF16), v.astype(BF16))
    return o, s_new


def _gla_scan_kernel(q_ref, k_ref, v_ref, la_ref, o_ref, sfin_ref, s_sc, *, n_chunks):
    t = pl.program_id(1)
    dk = q_ref.shape[-1] // GLA_HEADS
    dv = v_ref.shape[-1] // GLA_HEADS
    c = GLA_CHUNK

    @pl.when(t == 0)
    def _():
        s_sc[...] = jnp.zeros_like(s_sc)

    tri = (lax.broadcasted_iota(jnp.int32, (c, c), 0) >= lax.broadcasted_iota(jnp.int32, (c, c), 1))
    tri = jnp.where(tri, 1.0, 0.0).astype(BF16)
    rows = lax.broadcasted_iota(jnp.int32, (c, 1), 0)

    def chunk(ci, carry):
        r0 = pl.multiple_of(ci * c, c)
        b_all = _exact_left_mul(tri, la_ref[pl.ds(r0, c), :])
        for h in range(GLA_HEADS):
            o, s_new = _gla_chunk_head(
                q_ref[pl.ds(r0, c), h * dk:(h + 1) * dk],
                k_ref[pl.ds(r0, c), h * dk:(h + 1) * dk],
                v_ref[pl.ds(r0, c), h * dv:(h + 1) * dv],
                b_all[:, h * dk:(h + 1) * dk],
                s_sc[h], rows)
            o_ref[pl.ds(r0, c), h * dv:(h + 1) * dv] = o
            s_sc[h] = s_new
        return carry

    lax.fori_loop(0, n_chunks, chunk, 0)

    @pl.when(t == pl.num_programs(1) - 1)
    def _():
        sfin_ref[...] = s_sc[...]


def _gla_scan(q, k, v, la):
    bsz, length, dk_tot = q.shape
    dv_tot = v.shape[-1]
    dk, dv = dk_tot // GLA_HEADS, dv_tot // GLA_HEADS
    tt = min(SCAN_T, length)
    return pl.pallas_call(
        functools.partial(_gla_scan_kernel, n_chunks=tt // GLA_CHUNK),
        grid=(bsz, length // tt),
        in_specs=[_row_spec(tt, dk_tot), _row_spec(tt, dk_tot), _row_spec(tt, dv_tot), _row_spec(tt, dk_tot)],
        out_specs=[_row_spec(tt, dv_tot),
                   pl.BlockSpec((None, GLA_HEADS, dk, dv), lambda b, i: (b, 0, 0, 0))],
        out_shape=[jax.ShapeDtypeStruct((bsz, length, dv_tot), F32),
                   jax.ShapeDtypeStruct((bsz, GLA_HEADS, dk, dv), F32)],
        scratch_shapes=[pltpu.VMEM((GLA_HEADS, dk, dv), F32)],
        compiler_params=_params(("parallel", "arbitrary")),
        name="gla_scan",
    )(q, k, v, la)


def _gla_step_kernel(qt_ref, kt_ref, lat_ref, v_ref, s_ref, *rest, n_prev):
    if n_prev:
        prev_ref, o_ref, so_ref = rest
        so_ref[0:n_prev] = prev_ref[...]
    else:
        o_ref, so_ref = rest
    dk, dv = s_ref.shape[-2:]
    for g in range(s_ref.shape[0]):
        for h in range(GLA_HEADS):
            q_col = qt_ref[h * dk:(h + 1) * dk, g:g + 1]
            k_col = kt_ref[h * dk:(h + 1) * dk, g:g + 1]
            a_col = jnp.exp(lat_ref[h * dk:(h + 1) * dk, g:g + 1])
            v_row = v_ref[g:g + 1, h * dv:(h + 1) * dv]
            s_new = a_col * s_ref[g, h] + k_col * v_row
            so_ref[n_prev, g, h] = s_new
            o_ref[g:g + 1, h * dv:(h + 1) * dv] = jnp.sum(q_col * s_new, axis=0, keepdims=True)


def _gla_step(q, k, la, v, states, layer, prev=None):
    n, dk_tot = q.shape
    dv_tot = v.shape[-1]
    dk, dv = states.shape[-2:]
    n_prev = 0 if prev is None else prev.shape[0]
    g = GLA_STEP_G
    steps = n // g

    def cols(a):
        return a.reshape(steps, g, a.shape[-1]).transpose(0, 2, 1)

    col_spec = pl.BlockSpec((None, dk_tot, g), lambda i: (i, 0, 0))
    prev_in = [] if prev is None else [prev]
    prev_spec = [] if prev is None else [pl.BlockSpec((n_prev, g, GLA_HEADS, dk, dv), lambda i: (0, i, 0, 0, 0))]
    o, stacked = pl.pallas_call(
        functools.partial(_gla_step_kernel, n_prev=n_prev),
        grid=(steps,),
        in_specs=[col_spec, col_spec, col_spec,
                  pl.BlockSpec((None, g, dv_tot), lambda i: (i, 0, 0)),
                  pl.BlockSpec((None, g, GLA_HEADS, dk, dv), lambda i: (layer, i, 0, 0, 0))] + prev_spec,
        out_specs=[pl.BlockSpec((None, g, dv_tot), lambda i: (i, 0, 0)),
                   pl.BlockSpec((n_prev + 1, g, GLA_HEADS, dk, dv), lambda i: (0, i, 0, 0, 0))],
        out_shape=[jax.ShapeDtypeStruct((steps, g, dv_tot), F32),
                   jax.ShapeDtypeStruct((n_prev + 1,) + states.shape[1:], F32)],
        compiler_params=_params(("parallel",)),
        name="gla_step",
    )(cols(q), cols(k), cols(la), v.reshape(steps, g, dv_tot), states, *prev_in)
    return o.reshape(n, dv_tot), stacked


def _out_kernel(o_ref, g_ref, x_ref, gate_ref, w_ref, *rest, head_dim):
    if head_dim:
        onorm_ref, y_ref = rest
        o = o_ref[...]
        o = jnp.concatenate(
            [_rms(o[:, h * head_dim:(h + 1) * head_dim]) for h in range(o.shape[-1] // head_dim)], axis=-1)
        o = o * onorm_ref[...]
    else:
        (y_ref,) = rest
        o = o_ref[...]
    a = (o * _silu(g_ref[...])).astype(BF16)
    y_ref[...] = x_ref[...] + gate_ref[...] * _dot(a, w_ref[...])


def _out_proj(o, g, x, gate, w_out, onorm=None):
    bsz, length, d = x.shape
    width = o.shape[-1]
    tm = _row_tile(length)
    wb = w_out.astype(BF16)
    extra, head_dim = [], 0
    if onorm is not None:
        head_dim = onorm.shape[0]
        extra = [jnp.tile(onorm, width // head_dim).reshape(1, width)]
    return pl.pallas_call(
        functools.partial(_out_kernel, head_dim=head_dim),
        grid=(bsz, length // tm),
        in_specs=[_row_spec(tm, width), _row_spec(tm, width), _row_spec(tm, d), _mod_spec(gate, tm),
                  _full_spec(wb)] + [_full_spec(e) for e in extra],
        out_specs=_row_spec(tm, d),
        out_shape=jax.ShapeDtypeStruct((bsz, length, d), F32),
        compiler_params=_params(("parallel", "parallel")),
        name="out_proj",
    )(o, g, x, gate, wb, *extra)


def _kv_kernel(x_ref, sh_ref, sc_ref, wk_ref, wv_ref, wf_ref, wft_ref, bf_ref, bft_ref, kn_ref,
               kt_ref, vt_ref, lft_ref, lf_ref, kb_ref, vtb_ref):
    h = _modulated_norm(x_ref, sh_ref, sc_ref)
    k = _head_rms64(_dot(h, wk_ref[...])) * kn_ref[...]
    v = _dot(h, wv_ref[...])
    k_t = jnp.transpose(k)
    v_t = jnp.transpose(v)
    for hd in range(kt_ref.shape[0]):
        kt_ref[hd] = k_t[hd * FOX_HD:(hd + 1) * FOX_HD, :]
        vt_ref[hd] = v_t[hd * FOX_HD:(hd + 1) * FOX_HD, :]
    kb_ref[...] = k.astype(BF16)
    vtb_ref[...] = v_t.astype(BF16)
    lf_ref[...] = _log_sigmoid(_dot(h, wf_ref[...]) + bf_ref[...])
    lft_ref[...] = _log_sigmoid(_dot_nt(wft_ref[...], h) + bft_ref[...])


def _kv_proj(x, shift, scale, w_kv, b_f, k_norm):
    bsz, length, d = x.shape
    nh = b_f.shape[0]
    d_b = (w_kv.shape[1] - nh) // 2
    hd = d_b // nh
    wb = w_kv.astype(BF16)
    weights = (wb[:, :d_b], wb[:, d_b:2 * d_b], wb[:, 2 * d_b:], wb[:, 2 * d_b:].T,
               b_f.reshape(1, nh), b_f.reshape(nh, 1), jnp.tile(k_norm, nh).reshape(1, d_b))
    tm = _row_tile(length)
    tok_minor = pl.BlockSpec((None, nh, hd, tm), lambda b, i: (b, 0, 0, i))
    return pl.pallas_call(
        _kv_kernel,
        grid=(bsz, length // tm),
        in_specs=[_row_spec(tm, d), _mod_spec(shift, tm), _mod_spec(scale, tm)]
        + [_full_spec(w) for w in weights],
        out_specs=[tok_minor, tok_minor,
                   pl.BlockSpec((None, nh, tm), lambda b, i: (b, 0, i)),
                   _row_spec(tm, nh), _row_spec(tm, d_b),
                   pl.BlockSpec((None, d_b, tm), lambda b, i: (b, 0, i))],
        out_shape=[jax.ShapeDtypeStruct((bsz, nh, hd, length), F32),
                   jax.ShapeDtypeStruct((bsz, nh, hd, length), F32),
                   jax.ShapeDtypeStruct((bsz, nh, length), F32),
                   jax.ShapeDtypeStruct((bsz, length, nh), F32),
                   jax.ShapeDtypeStruct((bsz, length, d_b), BF16),
                   jax.ShapeDtypeStruct((bsz, d_b, length), BF16)],
        compiler_params=_params(("parallel", "parallel")),
        name="kv_proj",
    )(x, shift, scale, *weights)


def _fox_in_kernel(x_ref, sh_ref, sc_ref, wq_ref, wg_ref, qn_ref, q_ref, g_ref):
    h = _modulated_norm(x_ref, sh_ref, sc_ref)
    q_ref[...] = (_head_rms64(_dot(h, wq_ref[...])) * qn_ref[...]).astype(BF16)
    g_ref[...] = _dot(h, wg_ref[...])


def _fox_in(x, shift, scale, w_in, q_norm):
    bsz, length, d = x.shape
    d_b = w_in.shape[1] // 2
    wb = w_in.astype(BF16)
    qn = (jnp.tile(q_norm, d_b // FOX_HD) * (FOX_HD ** -0.5 * LOG2E)).reshape(1, d_b)
    weights = (wb[:, :d_b], wb[:, d_b:], qn)
    tm = _row_tile(length)
    return pl.pallas_call(
        _fox_in_kernel,
        grid=(bsz, length // tm),
        in_specs=[_row_spec(tm, d), _mod_spec(shift, tm), _mod_spec(scale, tm)]
        + [_full_spec(w) for w in weights],
        out_specs=[_row_spec(tm, d_b), _row_spec(tm, d_b)],
        out_shape=[jax.ShapeDtypeStruct((bsz, length, d_b), BF16),
                   jax.ShapeDtypeStruct((bsz, length, d_b), F32)],
        compiler_params=_params(("parallel", "parallel")),
        name="fox_in_proj",
    )(x, shift, scale, *weights)


def _cumsum_kernel(lf_ref, f_ref, carry_sc):
    r = lf_ref.shape[0]

    @pl.when(pl.program_id(1) == 0)
    def _():
        carry_sc[...] = jnp.zeros_like(carry_sc)

    tri = lax.broadcasted_iota(jnp.int32, (r, r), 0) >= lax.broadcasted_iota(jnp.int32, (r, r), 1)
    tri = jnp.where(tri, 1.0, 0.0).astype(BF16)
    f = _exact_left_mul(tri, lf_ref[...]) + carry_sc[...]
    f_ref[...] = f * LOG2E
    carry_sc[...] = f[r - 1:r, :]


def _cumsum_rows(lf):
    bsz, length, nh = lf.shape
    r = min(ROW_TILE, length)
    return pl.pallas_call(
        _cumsum_kernel,
        grid=(bsz, length // r),
        in_specs=[_row_spec(r, nh)],
        out_specs=_row_spec(r, nh),
        out_shape=jax.ShapeDtypeStruct(lf.shape, F32),
        scratch_shapes=[pltpu.VMEM((1, nh), F32)],
        compiler_params=_params(("parallel", "arbitrary")),
        name="forget_cumsum",
    )(lf)


N_PIECES = 3


def _key_aug_kernel(kb_ref, f_ref, place_k_ref, place_f_ref, ones_ref, ka_ref):
    bias = ones_ref[...]
    for j, piece in enumerate(_split3(f_ref[...])):
        bias = bias - _dot(piece, place_f_ref[j])
    for p in range(ka_ref.shape[0] // 2):
        pair = _dot(kb_ref[:, p * LANES:(p + 1) * LANES], place_k_ref[...]) + bias[:, 2 * p * LANES:2 * (p + 1) * LANES]
        ka_ref[2 * p] = pair[:, :LANES].astype(BF16)
        ka_ref[2 * p + 1] = pair[:, LANES:].astype(BF16)


def _key_aug(kb, f2):
    bsz, length, d_b = kb.shape
    nh = f2.shape[-1]
    tm = _row_tile(length)
    col = jnp.arange(nh * LANES)
    row = jnp.arange(LANES)
    place_k = ((col[None, :2 * LANES] // LANES == row[:, None] // FOX_HD)
               & (col[None, :2 * LANES] % LANES == row[:, None] % FOX_HD)).astype(BF16)
    head = jnp.arange(nh)
    place_f = jnp.stack([(col[None, :] == head[:, None] * LANES + FOX_HD + j) for j in range(N_PIECES)]
                        ).astype(BF16)
    ones = ((col % LANES >= FOX_HD + N_PIECES) & (col % LANES < FOX_HD + 2 * N_PIECES)).astype(F32)[None, :]
    return pl.pallas_call(
        _key_aug_kernel,
        grid=(bsz, length // tm),
        in_specs=[_row_spec(tm, d_b), _row_spec(tm, nh), _full_spec(place_k), _full_spec(place_f),
                  _full_spec(ones)],
        out_specs=pl.BlockSpec((None, nh, tm, LANES), lambda b, i: (b, 0, i, 0)),
        out_shape=jax.ShapeDtypeStruct((bsz, nh, length, LANES), BF16),
        compiler_params=_params(("parallel", "parallel")),
        name="fox_key_aug",
    )(kb, f2, place_k, place_f, ones)


def _fox_attn_kernel(pt_ref, q_ref, ka_ref, vt_ref, fq_ref, *rest, n_pages, steps_per_sample, out_offset):
    del pt_ref
    score_refs = rest[:2 * n_pages + 4]
    out_refs = rest[2 * n_pages + 4:3 * n_pages + 5]
    o_ref, od_ref, qt_sc, m_sc, l_sc, acc_sc, s_sc, st_sc = rest[3 * n_pages + 5:]
    step = (pl.program_id(0) * pl.num_programs(1) + pl.program_id(1)) * pl.num_programs(2) + pl.program_id(2)

    @pl.when(step % steps_per_sample == 0)
    def _():
        _decode_scores(score_refs, s_sc, st_sc, n_pages)

    @pl.when(step % steps_per_sample == out_offset)
    def _():
        _decode_output(out_refs, od_ref, s_sc, st_sc, n_pages)

    qi = pl.program_id(2)
    tq = q_ref.shape[0]
    tk = tq
    q_t = jnp.transpose(q_ref[...].astype(F32))
    brow = lax.broadcasted_iota(jnp.int32, (FOX_HD, 1), 0)
    ones_rows = jnp.broadcast_to(jnp.where(brow < N_PIECES, 1.0, 0.0), (FOX_HD, tq))

    def q_aug(i, c):
        extra = ones_rows
        if c is not None:
            for j, piece in enumerate(_split3(c)):
                extra = jnp.where(brow == N_PIECES + j, piece.astype(F32), extra)
        return jnp.concatenate([q_t[i * FOX_HD:(i + 1) * FOX_HD, :], extra], axis=0).astype(BF16)

    q_plain = (q_aug(0, None), q_aug(1, None))

    def reset():
        m_sc[...] = jnp.full_like(m_sc, NEG_BIG)
        l_sc[...] = jnp.zeros_like(l_sc)
        acc_sc[...] = jnp.zeros_like(acc_sc)

    def online_block(kj, causal):
        c0 = pl.multiple_of(kj * tk, tk)
        ts = []
        for i in range(2):
            t = _dot(ka_ref[i, pl.ds(c0, tk), :], q_plain[i])
            if causal:
                key = lax.broadcasted_iota(jnp.int32, (tk, tq), 0)
                qry = lax.broadcasted_iota(jnp.int32, (tk, tq), 1)
                t = jnp.where(key <= qry, t, NEG_BIG)
            ts.append(t)
        ps, alphas = [], []
        for i in range(2):
            fq = fq_ref[i:i + 1, :]
            m_old = m_sc[i]
            m_new = jnp.maximum(m_old, jnp.max(ts[i], axis=0, keepdims=True) + fq)
            alpha = jnp.exp2(m_old - m_new)
            p = jnp.exp2(ts[i] + (fq - m_new))
            l_sc[i] = alpha * l_sc[i] + jnp.sum(p, axis=0, keepdims=True)
            m_sc[i] = m_new
            ps.append(p.astype(BF16))
            alphas.append(alpha)
        for i in range(2):
            acc_sc[i] = alphas[i] * acc_sc[i] + _dot(vt_ref[i * FOX_HD:(i + 1) * FOX_HD, pl.ds(c0, tk)], ps[i])

    def result():
        return jnp.concatenate([acc_sc[0] / l_sc[0], acc_sc[1] / l_sc[1]], axis=0)

    l_sc[...] = jnp.zeros_like(l_sc)
    acc_sc[...] = jnp.zeros_like(acc_sc)
    q0 = pl.multiple_of(qi * tk, tk)
    for i in range(2):
        k_t = jnp.transpose(ka_ref[i, pl.ds(q0, tk), :].astype(F32))
        self_logit = jnp.sum(q_t[i * FOX_HD:(i + 1) * FOX_HD, :] * k_t[:FOX_HD, :], axis=0, keepdims=True)
        qt_sc[i] = q_aug(i, fq_ref[i:i + 1, :] - self_logit)

    def fixed_shift_blocks(kjs, causal=False):
        ps, sums = [], [0.0, 0.0]
        for kj in kjs:
            c0 = pl.multiple_of(kj * tk, tk)
            for i in range(2):
                x = _dot(ka_ref[i, pl.ds(c0, tk), :], qt_sc[i])
                if causal:
                    key = lax.broadcasted_iota(jnp.int32, (tk, tq), 0)
                    qry = lax.broadcasted_iota(jnp.int32, (tk, tq), 1)
                    x = jnp.where(key <= qry, x, NEG_BIG)
                p = jnp.exp2(x)
                sums[i] = sums[i] + jnp.sum(p, axis=0, keepdims=True)
                ps.append(p.astype(BF16))
        for i in range(2):
            l_sc[i] = l_sc[i] + sums[i]
            pv = 0.0
            for n, kj in enumerate(kjs):
                c0 = pl.multiple_of(kj * tk, tk)
                pv = pv + _dot(vt_ref[i * FOX_HD:(i + 1) * FOX_HD, pl.ds(c0, tk)], ps[2 * n + i])
            acc_sc[i] = acc_sc[i] + pv

    def quad_body(j, carry):
        fixed_shift_blocks(tuple(4 * j + n for n in range(4)))
        return carry

    lax.fori_loop(0, qi // 4, quad_body, 0)
    done = (qi // 4) * 4

    @pl.when(qi % 4 >= 2)
    def _():
        fixed_shift_blocks((done, done + 1))

    @pl.when(qi % 2 == 1)
    def _():
        fixed_shift_blocks((qi - 1,))

    fixed_shift_blocks((qi,), causal=True)
    o_t = result()
    bad = (jnp.sum(jnp.where(jnp.isfinite(o_t), 0.0, 1.0))
           + jnp.sum(jnp.where(jnp.isfinite(l_sc[...]), 0.0, 1.0)))

    @pl.when(bad == 0.0)
    def _():
        o_ref[...] = jnp.transpose(o_t)

    @pl.when(bad != 0.0)
    def _():
        reset()

        def body(kj, carry):
            online_block(kj, False)
            return carry

        lax.fori_loop(0, qi, body, 0)
        online_block(qi, True)
        o_ref[...] = jnp.transpose(result())


def _fox_attn(q, ka, vt, fq, q_s, k_new, v_new, lf_new, cache_k, cache_v, cache_logf, page_table):
    bsz, length, d_b = q.shape
    pairs = fq.shape[1]
    tq = min(ATTN_TK, length)
    nq = length // tq
    n, n_pages = page_table.shape
    _, ps, nh, hd = cache_k.shape
    steps_per_sample, rem = divmod(bsz * pairs * nq, n)
    assert rem == 0 and steps_per_sample >= 1, "prompt grid must be a multiple of the sample count"

    out_offset = min(1, steps_per_sample - 1)

    def score_sample(b, p, i):
        return ((b * pairs + p) * nq + i) // steps_per_sample

    def out_sample(b, p, i):
        return jnp.maximum((b * pairs + p) * nq + i - out_offset, 0) // steps_per_sample

    dec_operands, dec_specs, dec_out_spec = _decode_operands(
        q_s, k_new, v_new, lf_new, cache_k, cache_v, cache_logf, page_table, score_sample, out_sample)
    resident = pl.Buffered(1)
    grid_spec = pltpu.PrefetchScalarGridSpec(
        num_scalar_prefetch=1,
        grid=(bsz, pairs, nq),
        in_specs=[
            pl.BlockSpec((None, tq, LANES), lambda b, p, i, pt: (b, i, p)),
            pl.BlockSpec((None, 2, length, LANES), lambda b, p, i, pt: (b, p, 0, 0), pipeline_mode=resident),
            pl.BlockSpec((None, LANES, length), lambda b, p, i, pt: (b, p, 0), pipeline_mode=resident),
            pl.BlockSpec((None, None, 2, tq), lambda b, p, i, pt: (b, p, 0, i)),
        ] + dec_specs,
        out_specs=[pl.BlockSpec((None, tq, LANES), lambda b, p, i, pt: (b, i, p)), dec_out_spec],
        scratch_shapes=[pltpu.VMEM((2, LANES, tq), BF16),
                        pltpu.VMEM((2, 1, tq), F32), pltpu.VMEM((2, 1, tq), F32),
                        pltpu.VMEM((2, FOX_HD, tq), F32),
                        pltpu.VMEM((nh, n_pages * ps), F32), pltpu.VMEM((nh, LANES), F32)],
    )
    o_p, o_t = pl.pallas_call(
        functools.partial(_fox_attn_kernel, n_pages=n_pages, steps_per_sample=steps_per_sample,
                          out_offset=out_offset),
        grid_spec=grid_spec,
        out_shape=[jax.ShapeDtypeStruct((bsz, length, d_b), F32), jax.ShapeDtypeStruct((n, hd, nh), F32)],
        compiler_params=_params(("arbitrary", "arbitrary", "arbitrary")),
        name="fox_attn",
    )(page_table.reshape(-1), q, ka, vt, fq, *dec_operands)
    return o_p, o_t.transpose(0, 2, 1).reshape(n, nh * hd)


def _decode_scores(refs, s_sc, st_sc, n_pages):
    k_pages = refs[:n_pages]
    f_pages = refs[n_pages:2 * n_pages]
    qt_ref, q_ref, kn_ref, lfn_ref = refs[2 * n_pages:]
    nh, hd, ps = k_pages[0].shape

    later = lax.broadcasted_iota(jnp.int32, (ps, ps), 0) > lax.broadcasted_iota(jnp.int32, (ps, ps), 1)
    later = jnp.where(later, 1.0, 0.0).astype(BF16)
    carry = lfn_ref[...]
    for j in reversed(range(n_pages)):
        page = f_pages[j][...]
        within = sum(_dot(piece, later) for piece in _split3(page))
        s_sc[:, j * ps:(j + 1) * ps] = (within + carry) * LOG2E
        carry = carry + jnp.sum(page, axis=-1, keepdims=True)

    for h in range(nh):
        q_col = jnp.broadcast_to(qt_ref[:, h:h + 1].astype(F32), (hd, ps))
        for j in range(n_pages):
            cols = slice(j * ps, (j + 1) * ps)
            s_sc[h:h + 1, cols] = s_sc[h:h + 1, cols] + jnp.sum(k_pages[j][h] * q_col, axis=0, keepdims=True)
    s_self = jnp.sum(q_ref[...].astype(F32) * kn_ref[...], axis=-1, keepdims=True)
    s = s_sc[...]
    m = jnp.maximum(s_self, jnp.max(s, axis=-1, keepdims=True))
    p = jnp.exp2(s - m)
    p_self = jnp.exp2(s_self - m)
    denom = p_self + jnp.sum(p, axis=-1, keepdims=True)
    s_sc[...] = p
    st_sc[:, 0:1] = p_self
    st_sc[:, 1:2] = denom


def _decode_output(refs, o_ref, s_sc, st_sc, n_pages):
    v_pages = refs[:n_pages]
    vnt_ref = refs[n_pages]
    nh, hd, ps = v_pages[0].shape
    p_self = st_sc[:, 0:1]
    denom = st_sc[:, 1:2]
    head_lane = lax.broadcasted_iota(jnp.int32, (1, nh), 1)
    o_t = jnp.zeros((hd, nh), F32)
    for h in range(nh):
        acc = jnp.zeros((hd, ps), F32)
        for j in range(n_pages):
            acc = acc + v_pages[j][h] * s_sc[h:h + 1, j * ps:(j + 1) * ps]
        o_t = jnp.where(head_lane == h, jnp.sum(acc, axis=-1, keepdims=True), o_t)

    def as_row(col):
        eye = lax.broadcasted_iota(jnp.int32, (nh, nh), 0) == lax.broadcasted_iota(jnp.int32, (nh, nh), 1)
        return jnp.sum(jnp.where(eye, col, 0.0), axis=0, keepdims=True)

    o_ref[...] = (o_t + vnt_ref[...] * as_row(p_self)) / as_row(denom)


def _decode_operands(q, k_new, v_new, lf_new, cache_k, cache_v, cache_logf, page_table, score_sample, out_sample):
    n_pages = page_table.shape[1]
    _, ps, nh, hd = cache_k.shape
    k_view = cache_k.transpose(0, 2, 3, 1)
    v_view = cache_v.transpose(0, 2, 3, 1)
    f_view = cache_logf.transpose(0, 2, 1)

    def kv_spec(j, which):
        return pl.BlockSpec((None, nh, hd, ps), lambda *a: (a[-1][which(*a[:-1]) * n_pages + j], 0, 0, 0))

    def f_spec(j):
        return pl.BlockSpec((None, nh, ps), lambda *a: (a[-1][score_sample(*a[:-1]) * n_pages + j], 0, 0))

    def per_sample(shape, which):
        return pl.BlockSpec((None,) + shape, lambda *a: (which(*a[:-1]), 0, 0))

    specs = ([kv_spec(j, score_sample) for j in range(n_pages)] + [f_spec(j) for j in range(n_pages)]
             + [per_sample((hd, nh), score_sample), per_sample((nh, hd), score_sample),
                per_sample((nh, hd), score_sample), per_sample((nh, 1), score_sample)]
             + [kv_spec(j, out_sample) for j in range(n_pages)] + [per_sample((hd, nh), out_sample)])
    operands = ([k_view] * n_pages + [f_view] * n_pages + [q.transpose(0, 2, 1), q, k_new, lf_new[:, :, None]]
                + [v_view] * n_pages + [v_new.transpose(0, 2, 1)])
    return operands, specs, per_sample((hd, nh), out_sample)


def kernel(x_prompt, x_sample, c_prompt, c_sample, state_gla, cache_k, cache_v, cache_logf, page_table,
           ada_w, ada_b, gla_w_in, gla_w_g2, gla_b_g, gla_onorm, gla_w_out, kv_ada_w, kv_ada_b, w_kv, b_f,
           k_norm, fox_w_in, q_norm, fox_w_out):
    bp, seq, d = x_prompt.shape
    ns = x_sample.shape[0]
    n_a = gla_w_in.shape[0]
    depth = ada_w.shape[0]
    nh = b_f.shape[0]
    d_b = nh * FOX_HD
    dk, dv = state_gla.shape[-2:]

    pad = (-(ns + bp)) % 8
    c_all = jnp.concatenate([c_sample, c_prompt, jnp.zeros((pad, d), F32)], axis=0)
    mod = _ada(c_all, ada_w, ada_b)
    kv_mod = _ada(c_all, kv_ada_w[None], kv_ada_b[None])[0]

    def split_mod(m, n_parts):
        parts = jnp.split(m, n_parts, axis=-1)
        return ([p[ns:ns + bp, None, :] for p in parts],
                [p[None, :ns, :] for p in parts])

    xp = x_prompt
    xs = x_sample.reshape(1, ns, d)
    sp_states, ss_states = [], None
    for layer in range(n_a):
        mod_p, mod_s = split_mod(mod[layer], 3)
        q, k, v, g, la = _gla_in(xp, mod_p[0], mod_p[1], gla_w_in[layer], gla_w_g2[layer], gla_b_g[layer])
        o, s_fin = _gla_scan(q, k, v, la)
        sp_states.append(s_fin)
        xp = _out_proj(o, g, xp, mod_p[2], gla_w_out[layer], gla_onorm[layer])
        q, k, v, g, la = _gla_in(xs, mod_s[0], mod_s[1], gla_w_in[layer], gla_w_g2[layer], gla_b_g[layer])
        o, ss_states = _gla_step(q[0], k[0], la[0], v[0], state_gla, layer, ss_states)
        xs = _out_proj(o.reshape(1, ns, -1), g, xs, mod_s[2], gla_w_out[layer], gla_onorm[layer])

    kvm_p, kvm_s = split_mod(kv_mod, 2)
    kt_p, vt_p, lft_p, lf_p, kb_p, vtb_p = _kv_proj(xp, kvm_p[0], kvm_p[1], w_kv, b_f, k_norm)
    kt_s, vt_s, lft_s, _, _, _ = _kv_proj(xs, kvm_s[0], kvm_s[1], w_kv, b_f, k_norm)
    f2_p = _cumsum_rows(lf_p)
    ka_p = _key_aug(kb_p, f2_p)
    fq_p = f2_p.reshape(bp, seq, nh // 2, 2).transpose(0, 2, 3, 1)
    k_new = kt_s[0].transpose(2, 0, 1)
    v_new = vt_s[0].transpose(2, 0, 1)
    lf_new = lft_s[0].T

    for j in range(depth - n_a):
        mod_p, mod_s = split_mod(mod[n_a + j], 3)
        q_p, g_p = _fox_in(xp, mod_p[0], mod_p[1], fox_w_in[j], q_norm[j])
        q_s, g_s = _fox_in(xs, mod_s[0], mod_s[1], fox_w_in[j], q_norm[j])
        o_p, o_s = _fox_attn(q_p, ka_p, vtb_p, fq_p, q_s.reshape(ns, nh, FOX_HD), k_new, v_new, lf_new,
                             cache_k, cache_v, cache_logf, page_table)
        xp = _out_proj(o_p, g_p, xp, mod_p[2], fox_w_out[j])
        xs = _out_proj(o_s.reshape(1, ns, d_b), g_s, xs, mod_s[2], fox_w_out[j])

    return (xp, xs.reshape(ns, 1, d),
            jnp.stack(sp_states), ss_states,
            kt_p.transpose(0, 3, 1, 2), vt_p.transpose(0, 3, 1, 2), lft_p.transpose(0, 2, 1),
            k_new[:, None], v_new[:, None], lf_new[:, None])
```

```python
import functools

import jax
import jax.numpy as jnp
from jax import lax
from jax.experimental import pallas as pl
from jax.experimental.pallas import tpu as pltpu

F32 = jnp.float32
BF16 = jnp.bfloat16

EPS = 1e-6
GATE_TAU = 16.0
GLA_HEADS = 4
GLA_CHUNK = 64
GLA_SUB = 16
FOX_HD = 64
LANES = 128
NEG_BIG = -1e30
LOG2E = 1.4426950408889634
VMEM_LIMIT = 56 * 1024 * 1024

ROW_TILE = 512
ATTN_TK = 512
SCAN_T = 512
GLA_STEP_G = 4


def _params(sem):
    return pltpu.CompilerParams(dimension_semantics=sem, vmem_limit_bytes=VMEM_LIMIT)


def _silu(x):
    return x / (1.0 + jnp.exp(-x))


def _log_sigmoid(x):
    return jnp.minimum(x, 0.0) - jnp.log1p(jnp.exp(-jnp.abs(x)))


def _rms(x):
    return x * lax.rsqrt(jnp.mean(x * x, axis=-1, keepdims=True) + EPS)


def _dot(a, b):
    return jnp.dot(a, b, preferred_element_type=F32)


def _dot_nt(a, b):
    return lax.dot_general(a, b, (((1,), (1,)), ((), ())), preferred_element_type=F32)


def _dot_tn(a, b):
    return lax.dot_general(a, b, (((0,), (0,)), ((), ())), preferred_element_type=F32)


def _split3(x):
    p0 = x.astype(BF16)
    r1 = x - p0.astype(F32)
    p1 = r1.astype(BF16)
    p2 = (r1 - p1.astype(F32)).astype(BF16)
    return p0, p1, p2


def _exact_left_mul(mat01, x):
    p0, p1, p2 = _split3(x)
    return _dot(mat01, p0) + _dot(mat01, p1) + _dot(mat01, p2)


def _modulated_norm(x_ref, shift_ref, scale_ref):
    h = _rms(x_ref[...]) * (1.0 + scale_ref[...]) + shift_ref[...]
    return h.astype(BF16)


def _head_rms64(p):
    lane = lax.broadcasted_iota(jnp.int32, (1, LANES), 1)
    low = lane < FOX_HD
    outs = []
    for c in range(p.shape[-1] // LANES):
        blk = p[:, c * LANES:(c + 1) * LANES]
        sq = blk * blk
        s_lo = jnp.sum(jnp.where(low, sq, 0.0), axis=-1, keepdims=True)
        s_hi = jnp.sum(jnp.where(low, 0.0, sq), axis=-1, keepdims=True)
        r_lo = lax.rsqrt(s_lo * (1.0 / FOX_HD) + EPS)
        r_hi = lax.rsqrt(s_hi * (1.0 / FOX_HD) + EPS)
        outs.append(blk * jnp.where(low, r_lo, r_hi))
    return jnp.concatenate(outs, axis=-1)


def _ada_kernel(c_ref, w_ref, b_ref, o_ref):
    a = _silu(c_ref[...]).astype(BF16)
    o_ref[...] = _dot(a, w_ref[...].astype(BF16)) + b_ref[...]


def _ada(c_all, w, b, tn=1024):
    nl, d, n = w.shape
    rows = c_all.shape[0]
    return pl.pallas_call(
        _ada_kernel,
        grid=(nl, n // tn),
        in_specs=[
            pl.BlockSpec((rows, d), lambda l, j: (0, 0)),
            pl.BlockSpec((None, d, tn), lambda l, j: (l, 0, j)),
            pl.BlockSpec((None, 1, tn), lambda l, j: (l, 0, j)),
        ],
        out_specs=pl.BlockSpec((None, rows, tn), lambda l, j: (l, 0, j)),
        out_shape=jax.ShapeDtypeStruct((nl, rows, n), F32),
        compiler_params=_params(("parallel", "parallel")),
        name="ada_mod",
    )(c_all, w, b.reshape(nl, 1, n))


def _row_spec(tm, width):
    return pl.BlockSpec((None, tm, width), lambda b, i: (b, i, 0))


def _mod_spec(mod, tm):
    if mod.shape[1] == 1:
        return pl.BlockSpec((None, 1, mod.shape[2]), lambda b, i: (b, 0, 0))
    return pl.BlockSpec((None, tm, mod.shape[2]), lambda b, i: (b, i, 0))


def _full_spec(a):
    nd = a.ndim
    return pl.BlockSpec(a.shape, lambda b, i: (0,) * nd)


def _row_tile(length):
    return min(ROW_TILE, length)


def _gla_in_kernel(x_ref, sh_ref, sc_ref, wq_ref, wk_ref, wv_ref, wg_ref, wz_ref, wg2_ref, bg_ref,
                   q_ref, k_ref, v_ref, g_ref, la_ref, *, q_scale):
    h = _modulated_norm(x_ref, sh_ref, sc_ref)
    q_ref[...] = _dot(h, wq_ref[...]) * q_scale
    k_ref[...] = _dot(h, wk_ref[...])
    v_ref[...] = _dot(h, wv_ref[...])
    g_ref[...] = _dot(h, wg_ref[...])
    z = _dot(h, wz_ref[...])
    t = _dot(z.astype(BF16), wg2_ref[...]) + bg_ref[...]
    la_ref[...] = _log_sigmoid(t) * (1.0 / GATE_TAU)


def _gla_in(x, shift, scale, w_in, w_g2, b_g):
    bsz, length, d = x.shape
    dk_tot = w_g2.shape[1]
    rank = w_g2.shape[0]
    dv_tot = (w_in.shape[1] - 2 * dk_tot - rank) // 2
    dk = dk_tot // GLA_HEADS
    wb = w_in.astype(BF16)
    wq = wb[:, :dk_tot]
    wk = wb[:, dk_tot:2 * dk_tot]
    wv = wb[:, 2 * dk_tot:2 * dk_tot + dv_tot]
    wg = wb[:, 2 * dk_tot + dv_tot:2 * dk_tot + 2 * dv_tot]
    wz = wb[:, 2 * dk_tot + 2 * dv_tot:]
    wg2 = w_g2.astype(BF16)
    bg = b_g.reshape(1, dk_tot)
    tm = _row_tile(length)
    weights = (wq, wk, wv, wg, wz, wg2, bg)
    widths = (dk_tot, dk_tot, dv_tot, dv_tot, dk_tot)
    return pl.pallas_call(
        functools.partial(_gla_in_kernel, q_scale=dk ** -0.5),
        grid=(bsz, length // tm),
        in_specs=[_row_spec(tm, d), _mod_spec(shift, tm), _mod_spec(scale, tm)]
        + [_full_spec(w) for w in weights],
        out_specs=[_row_spec(tm, w) for w in widths],
        out_shape=[jax.ShapeDtypeStruct((bsz, length, w), F32) for w in widths],
        compiler_params=_params(("parallel", "parallel")),
        name="gla_in_proj",
    )(x, shift, scale, *weights)


def _gla_chunk_head(q, k, v, b, s_old, rows):
    c, dk = q.shape
    nsub = c // GLA_SUB
    b_last = b[c - 1:c, :]
    o = _dot((q * jnp.exp(b)).astype(BF16), s_old.astype(BF16))
    kd = k * jnp.exp(b_last - b)
    upd = _dot_tn(kd.astype(BF16), v.astype(BF16))
    decay_col = jnp.exp(jnp.transpose(jnp.broadcast_to(b_last, (dk, dk))))
    s_new = jnp.concatenate([decay_col] * (s_old.shape[1] // dk), axis=1) * s_old + upd
    qcat, kcat = [], []
    for j in range(nsub - 1):
        r_j = b[(j + 1) * GLA_SUB - 1:(j + 1) * GLA_SUB, :]
        in_j = (rows >= j * GLA_SUB) & (rows < (j + 1) * GLA_SUB)
        after_j = rows >= (j + 1) * GLA_SUB
        kcat.append(jnp.where(in_j, k * jnp.exp(jnp.where(in_j, r_j - b, 0.0)), 0.0))
        qcat.append(jnp.where(after_j, q * jnp.exp(jnp.where(after_j, b - r_j, 0.0)), 0.0))
    att = _dot_nt(jnp.concatenate(qcat, axis=1).astype(BF16), jnp.concatenate(kcat, axis=1).astype(BF16))
    sub_row = lax.broadcasted_iota(jnp.int32, (GLA_SUB, 1), 0)
    col_id = lax.broadcasted_iota(jnp.int32, (1, c), 1)
    diag_blocks = []
    for j in range(nsub):
        lo = j * GLA_SUB
        qj = q[lo:lo + GLA_SUB, :]
        bj = b[lo:lo + GLA_SUB, :]
        blk = jnp.zeros((GLA_SUB, c), F32)
        for t in range(GLA_SUB):
            src = lo + t
            valid = sub_row >= t
            e = jnp.exp(jnp.where(valid, bj - b[src:src + 1, :], 0.0))
            col = jnp.sum(qj * k[src:src + 1, :] * e, axis=-1, keepdims=True)
            blk = jnp.where((col_id == src) & valid, col, blk)
        diag_blocks.append(blk)
    att = att + jnp.concatenate(diag_blocks, axis=0)
    o = o + _dot(att.astype(BF16), v.astype(BF16))
    return o, s_new


def _gla_scan_kernel(q_ref, k_ref, v_ref, la_ref, o_ref, sfin_ref, s_sc, *, n_chunks):
    t = pl.program_id(1)
    dk = q_ref.shape[-1] // GLA_HEADS
    dv = v_ref.shape[-1] // GLA_HEADS
    c = GLA_CHUNK

    @pl.when(t == 0)
    def _():
        s_sc[...] = jnp.zeros_like(s_sc)

    tri = (lax.broadcasted_iota(jnp.int32, (c, c), 0) >= lax.broadcasted_iota(jnp.int32, (c, c), 1))
    tri = jnp.where(tri, 1.0, 0.0).astype(BF16)
    rows = lax.broadcasted_iota(jnp.int32, (c, 1), 0)

    def chunk(ci, carry):
        r0 = pl.multiple_of(ci * c, c)
        b_all = _exact_left_mul(tri, la_ref[pl.ds(r0, c), :])
        for h in range(GLA_HEADS):
            o, s_new = _gla_chunk_head(
                q_ref[pl.ds(r0, c), h * dk:(h + 1) * dk],
                k_ref[pl.ds(r0, c), h * dk:(h + 1) * dk],
                v_ref[pl.ds(r0, c), h * dv:(h + 1) * dv],
                b_all[:, h * dk:(h + 1) * dk],
                s_sc[h], rows)
            o_ref[pl.ds(r0, c), h * dv:(h + 1) * dv] = o
            s_sc[h] = s_new
        return carry

    lax.fori_loop(0, n_chunks, chunk, 0)

    @pl.when(t == pl.num_programs(1) - 1)
    def _():
        sfin_ref[...] = s_sc[...]


def _gla_scan(q, k, v, la):
    bsz, length, dk_tot = q.shape
    dv_tot = v.shape[-1]
    dk, dv = dk_tot // GLA_HEADS, dv_tot // GLA_HEADS
    tt = min(SCAN_T, length)
    return pl.pallas_call(
        functools.partial(_gla_scan_kernel, n_chunks=tt // GLA_CHUNK),
        grid=(bsz, length // tt),
        in_specs=[_row_spec(tt, dk_tot), _row_spec(tt, dk_tot), _row_spec(tt, dv_tot), _row_spec(tt, dk_tot)],
        out_specs=[_row_spec(tt, dv_tot),
                   pl.BlockSpec((None, GLA_HEADS, dk, dv), lambda b, i: (b, 0, 0, 0))],
        out_shape=[jax.ShapeDtypeStruct((bsz, length, dv_tot), F32),
                   jax.ShapeDtypeStruct((bsz, GLA_HEADS, dk, dv), F32)],
        scratch_shapes=[pltpu.VMEM((GLA_HEADS, dk, dv), F32)],
        compiler_params=_params(("parallel", "arbitrary")),
        name="gla_scan",
    )(q, k, v, la)


def _gla_step_kernel(qt_ref, kt_ref, lat_ref, v_ref, s_ref, *rest, n_prev):
    if n_prev:
        prev_ref, o_ref, so_ref = rest
        so_ref[0:n_prev] = prev_ref[...]
    else:
        o_ref, so_ref = rest
    dk, dv = s_ref.shape[-2:]
    for g in range(s_ref.shape[0]):
        for h in range(GLA_HEADS):
            q_col = qt_ref[h * dk:(h + 1) * dk, g:g + 1]
            k_col = kt_ref[h * dk:(h + 1) * dk, g:g + 1]
            a_col = jnp.exp(lat_ref[h * dk:(h + 1) * dk, g:g + 1])
            v_row = v_ref[g:g + 1, h * dv:(h + 1) * dv]
            s_new = a_col * s_ref[g, h] + k_col * v_row
            so_ref[n_prev, g, h] = s_new
            o_ref[g:g + 1, h * dv:(h + 1) * dv] = jnp.sum(q_col * s_new, axis=0, keepdims=True)


def _gla_step(q, k, la, v, states, layer, prev=None):
    n, dk_tot = q.shape
    dv_tot = v.shape[-1]
    dk, dv = states.shape[-2:]
    n_prev = 0 if prev is None else prev.shape[0]
    g = GLA_STEP_G
    steps = n // g

    def cols(a):
        return a.reshape(steps, g, a.shape[-1]).transpose(0, 2, 1)

    col_spec = pl.BlockSpec((None, dk_tot, g), lambda i: (i, 0, 0))
    prev_in = [] if prev is None else [prev]
    prev_spec = [] if prev is None else [pl.BlockSpec((n_prev, g, GLA_HEADS, dk, dv), lambda i: (0, i, 0, 0, 0))]
    o, stacked = pl.pallas_call(
        functools.partial(_gla_step_kernel, n_prev=n_prev),
        grid=(steps,),
        in_specs=[col_spec, col_spec, col_spec,
                  pl.BlockSpec((None, g, dv_tot), lambda i: (i, 0, 0)),
                  pl.BlockSpec((None, g, GLA_HEADS, dk, dv), lambda i: (layer, i, 0, 0, 0))] + prev_spec,
        out_specs=[pl.BlockSpec((None, g, dv_tot), lambda i: (i, 0, 0)),
                   pl.BlockSpec((n_prev + 1, g, GLA_HEADS, dk, dv), lambda i: (0, i, 0, 0, 0))],
        out_shape=[jax.ShapeDtypeStruct((steps, g, dv_tot), F32),
                   jax.ShapeDtypeStruct((n_prev + 1,) + states.shape[1:], F32)],
        compiler_params=_params(("parallel",)),
        name="gla_step",
    )(cols(q), cols(k), cols(la), v.reshape(steps, g, dv_tot), states, *prev_in)
    return o.reshape(n, dv_tot), stacked


def _out_kernel(o_ref, g_ref, x_ref, gate_ref, w_ref, *rest, head_dim):
    if head_dim:
        onorm_ref, y_ref = rest
        o = o_ref[...]
        o = jnp.concatenate(
            [_rms(o[:, h * head_dim:(h + 1) * head_dim]) for h in range(o.shape[-1] // head_dim)], axis=-1)
        o = o * onorm_ref[...]
    else:
        (y_ref,) = rest
        o = o_ref[...]
    a = (o * _silu(g_ref[...])).astype(BF16)
    y_ref[...] = x_ref[...] + gate_ref[...] * _dot(a, w_ref[...])


def _out_proj(o, g, x, gate, w_out, onorm=None):
    bsz, length, d = x.shape
    width = o.shape[-1]
    tm = _row_tile(length)
    wb = w_out.astype(BF16)
    extra, head_dim = [], 0
    if onorm is not None:
        head_dim = onorm.shape[0]
        extra = [jnp.tile(onorm, width // head_dim).reshape(1, width)]
    return pl.pallas_call(
        functools.partial(_out_kernel, head_dim=head_dim),
        grid=(bsz, length // tm),
        in_specs=[_row_spec(tm, width), _row_spec(tm, width), _row_spec(tm, d), _mod_spec(gate, tm),
                  _full_spec(wb)] + [_full_spec(e) for e in extra],
        out_specs=_row_spec(tm, d),
        out_shape=jax.ShapeDtypeStruct((bsz, length, d), F32),
        compiler_params=_params(("parallel", "parallel")),
        name="out_proj",
    )(o, g, x, gate, wb, *extra)


def _kv_kernel(x_ref, sh_ref, sc_ref, wk_ref, wv_ref, wf_ref, wft_ref, bf_ref, bft_ref, kn_ref,
               kt_ref, vt_ref, lft_ref, lf_ref, kb_ref, vtb_ref):
    h = _modulated_norm(x_ref, sh_ref, sc_ref)
    k = _head_rms64(_dot(h, wk_ref[...])) * kn_ref[...]
    v = _dot(h, wv_ref[...])
    k_t = jnp.transpose(k)
    v_t = jnp.transpose(v)
    for hd in range(kt_ref.shape[0]):
        kt_ref[hd] = k_t[hd * FOX_HD:(hd + 1) * FOX_HD, :]
        vt_ref[hd] = v_t[hd * FOX_HD:(hd + 1) * FOX_HD, :]
    kb_ref[...] = k.astype(BF16)
    vtb_ref[...] = v_t.astype(BF16)
    lf_ref[...] = _log_sigmoid(_dot(h, wf_ref[...]) + bf_ref[...])
    lft_ref[...] = _log_sigmoid(_dot_nt(wft_ref[...], h) + bft_ref[...])


def _kv_proj(x, shift, scale, w_kv, b_f, k_norm):
    bsz, length, d = x.shape
    nh = b_f.shape[0]
    d_b = (w_kv.shape[1] - nh) // 2
    hd = d_b // nh
    wb = w_kv.astype(BF16)
    weights = (wb[:, :d_b], wb[:, d_b:2 * d_b], wb[:, 2 * d_b:], wb[:, 2 * d_b:].T,
               b_f.reshape(1, nh), b_f.reshape(nh, 1), jnp.tile(k_norm, nh).reshape(1, d_b))
    tm = _row_tile(length)
    tok_minor = pl.BlockSpec((None, nh, hd, tm), lambda b, i: (b, 0, 0, i))
    return pl.pallas_call(
        _kv_kernel,
        grid=(bsz, length // tm),
        in_specs=[_row_spec(tm, d), _mod_spec(shift, tm), _mod_spec(scale, tm)]
        + [_full_spec(w) for w in weights],
        out_specs=[tok_minor, tok_minor,
                   pl.BlockSpec((None, nh, tm), lambda b, i: (b, 0, i)),
                   _row_spec(tm, nh), _row_spec(tm, d_b),
                   pl.BlockSpec((None, d_b, tm), lambda b, i: (b, 0, i))],
        out_shape=[jax.ShapeDtypeStruct((bsz, nh, hd, length), F32),
                   jax.ShapeDtypeStruct((bsz, nh, hd, length), F32),
                   jax.ShapeDtypeStruct((bsz, nh, length), F32),
                   jax.ShapeDtypeStruct((bsz, length, nh), F32),
                   jax.ShapeDtypeStruct((bsz, length, d_b), BF16),
                   jax.ShapeDtypeStruct((bsz, d_b, length), BF16)],
        compiler_params=_params(("parallel", "parallel")),
        name="kv_proj",
    )(x, shift, scale, *weights)


def _fox_in_kernel(x_ref, sh_ref, sc_ref, wq_ref, wg_ref, qn_ref, q_ref, g_ref):
    h = _modulated_norm(x_ref, sh_ref, sc_ref)
    q_ref[...] = (_head_rms64(_dot(h, wq_ref[...])) * qn_ref[...]).astype(BF16)
    g_ref[...] = _dot(h, wg_ref[...])


def _fox_in(x, shift, scale, w_in, q_norm):
    bsz, length, d = x.shape
    d_b = w_in.shape[1] // 2
    wb = w_in.astype(BF16)
    qn = (jnp.tile(q_norm, d_b // FOX_HD) * (FOX_HD ** -0.5 * LOG2E)).reshape(1, d_b)
    weights = (wb[:, :d_b], wb[:, d_b:], qn)
    tm = _row_tile(length)
    return pl.pallas_call(
        _fox_in_kernel,
        grid=(bsz, length // tm),
        in_specs=[_row_spec(tm, d), _mod_spec(shift, tm), _mod_spec(scale, tm)]
        + [_full_spec(w) for w in weights],
        out_specs=[_row_spec(tm, d_b), _row_spec(tm, d_b)],
        out_shape=[jax.ShapeDtypeStruct((bsz, length, d_b), BF16),
                   jax.ShapeDtypeStruct((bsz, length, d_b), F32)],
        compiler_params=_params(("parallel", "parallel")),
        name="fox_in_proj",
    )(x, shift, scale, *weights)


def _cumsum_kernel(lf_ref, f_ref, carry_sc):
    r = lf_ref.shape[0]

    @pl.when(pl.program_id(1) == 0)
    def _():
        carry_sc[...] = jnp.zeros_like(carry_sc)

    tri = lax.broadcasted_iota(jnp.int32, (r, r), 0) >= lax.broadcasted_iota(jnp.int32, (r, r), 1)
    tri = jnp.where(tri, 1.0, 0.0).astype(BF16)
    f = _exact_left_mul(tri, lf_ref[...]) + carry_sc[...]
    f_ref[...] = f * LOG2E
    carry_sc[...] = f[r - 1:r, :]


def _cumsum_rows(lf):
    bsz, length, nh = lf.shape
    r = min(ROW_TILE, length)
    return pl.pallas_call(
        _cumsum_kernel,
        grid=(bsz, length // r),
        in_specs=[_row_spec(r, nh)],
        out_specs=_row_spec(r, nh),
        out_shape=jax.ShapeDtypeStruct(lf.shape, F32),
        scratch_shapes=[pltpu.VMEM((1, nh), F32)],
        compiler_params=_params(("parallel", "arbitrary")),
        name="forget_cumsum",
    )(lf)


N_PIECES = 3


def _key_aug_kernel(kb_ref, f_ref, place_k_ref, place_f_ref, ones_ref, ka_ref):
    bias = ones_ref[...]
    for j, piece in enumerate(_split3(f_ref[...])):
        bias = bias - _dot(piece, place_f_ref[j])
    for p in range(ka_ref.shape[0] // 2):
        pair = _dot(kb_ref[:, p * LANES:(p + 1) * LANES], place_k_ref[...]) + bias[:, 2 * p * LANES:2 * (p + 1) * LANES]
        ka_ref[2 * p] = pair[:, :LANES].astype(BF16)
        ka_ref[2 * p + 1] = pair[:, LANES:].astype(BF16)


def _key_aug(kb, f2):
    bsz, length, d_b = kb.shape
    nh = f2.shape[-1]
    tm = _row_tile(length)
    col = jnp.arange(nh * LANES)
    row = jnp.arange(LANES)
    place_k = ((col[None, :2 * LANES] // LANES == row[:, None] // FOX_HD)
               & (col[None, :2 * LANES] % LANES == row[:, None] % FOX_HD)).astype(BF16)
    head = jnp.arange(nh)
    place_f = jnp.stack([(col[None, :] == head[:, None] * LANES + FOX_HD + j) for j in range(N_PIECES)]
                        ).astype(BF16)
    ones = ((col % LANES >= FOX_HD + N_PIECES) & (col % LANES < FOX_HD + 2 * N_PIECES)).astype(F32)[None, :]
    return pl.pallas_call(
        _key_aug_kernel,
        grid=(bsz, length // tm),
        in_specs=[_row_spec(tm, d_b), _row_spec(tm, nh), _full_spec(place_k), _full_spec(place_f),
                  _full_spec(ones)],
        out_specs=pl.BlockSpec((None, nh, tm, LANES), lambda b, i: (b, 0, i, 0)),
        out_shape=jax.ShapeDtypeStruct((bsz, nh, length, LANES), BF16),
        compiler_params=_params(("parallel", "parallel")),
        name="fox_key_aug",
    )(kb, f2, place_k, place_f, ones)


def _page_copies(pt_ref, sample, src_hbm, buf, sem):
    n_pages = buf.shape[0]
    return [pltpu.make_async_copy(src_hbm.at[pt_ref[sample * n_pages + j]], buf.at[j], sem)
            for j in range(n_pages)]


def _fox_attn_kernel(pt_ref, q_ref, ka_ref, vt_ref, fq_ref, *rest, n_samples, steps_per_sample):
    kc_hbm, vc_hbm, fc_hbm, qt_ref, qs_ref, kn_ref, lfn_ref, vnt_ref = rest[:N_DEC_OPERANDS]
    o_ref, od_ref, qt_sc, m_sc, l_sc, acc_sc, s_sc, st_sc, kbuf, vbuf, fbuf, sems = rest[N_DEC_OPERANDS:]
    n_pages = kbuf.shape[0]
    step = (pl.program_id(0) * pl.num_programs(1) + pl.program_id(1)) * pl.num_programs(2) + pl.program_id(2)
    sample = step // steps_per_sample
    phase = step % steps_per_sample

    def score_copies(s):
        return _page_copies(pt_ref, s, kc_hbm, kbuf, sems.at[0]) + _page_copies(pt_ref, s, fc_hbm, fbuf, sems.at[1])

    def value_copies(s):
        return _page_copies(pt_ref, s, vc_hbm, vbuf, sems.at[2])

    @pl.when(step == 0)
    def _():
        for cp in score_copies(0):
            cp.start()

    @pl.when(phase == 0)
    def _():
        for cp in value_copies(sample):
            cp.start()
        for cp in score_copies(sample):
            cp.wait()
        _decode_scores([kbuf.at[j] for j in range(n_pages)] + [fbuf.at[j] for j in range(n_pages)]
                       + [qt_ref, qs_ref, kn_ref, lfn_ref], s_sc, st_sc, n_pages)

    @pl.when(phase == 1)
    def _():
        @pl.when(sample + 1 < n_samples)
        def _():
            for cp in score_copies(sample + 1):
                cp.start()
        for cp in value_copies(sample):
            cp.wait()
        _decode_output([vbuf.at[j] for j in range(n_pages)] + [vnt_ref], od_ref, s_sc, st_sc, n_pages)

    qi = pl.program_id(2)
    tq = q_ref.shape[0]
    tk = tq
    q_t = jnp.transpose(q_ref[...].astype(F32))
    brow = lax.broadcasted_iota(jnp.int32, (FOX_HD, 1), 0)
    ones_rows = jnp.broadcast_to(jnp.where(brow < N_PIECES, 1.0, 0.0), (FOX_HD, tq))

    def q_aug(i, c):
        extra = ones_rows
        if c is not None:
            for j, piece in enumerate(_split3(c)):
                extra = jnp.where(brow == N_PIECES + j, piece.astype(F32), extra)
        return jnp.concatenate([q_t[i * FOX_HD:(i + 1) * FOX_HD, :], extra], axis=0).astype(BF16)

    q_plain = (q_aug(0, None), q_aug(1, None))

    def reset():
        m_sc[...] = jnp.full_like(m_sc, NEG_BIG)
        l_sc[...] = jnp.zeros_like(l_sc)
        acc_sc[...] = jnp.zeros_like(acc_sc)

    def online_block(kj, causal):
        c0 = pl.multiple_of(kj * tk, tk)
        ts = []
        for i in range(2):
            t = _dot(ka_ref[i, pl.ds(c0, tk), :], q_plain[i])
            if causal:
                key = lax.broadcasted_iota(jnp.int32, (tk, tq), 0)
                qry = lax.broadcasted_iota(jnp.int32, (tk, tq), 1)
                t = jnp.where(key <= qry, t, NEG_BIG)
            ts.append(t)
        ps, alphas = [], []
        for i in range(2):
            fq = fq_ref[i:i + 1, :]
            m_old = m_sc[i]
            m_new = jnp.maximum(m_old, jnp.max(ts[i], axis=0, keepdims=True) + fq)
            alpha = jnp.exp2(m_old - m_new)
            p = jnp.exp2(ts[i] + (fq - m_new))
            l_sc[i] = alpha * l_sc[i] + jnp.sum(p, axis=0, keepdims=True)
            m_sc[i] = m_new
            ps.append(p.astype(BF16))
            alphas.append(alpha)
        for i in range(2):
            acc_sc[i] = alphas[i] * acc_sc[i] + _dot(vt_ref[i * FOX_HD:(i + 1) * FOX_HD, pl.ds(c0, tk)], ps[i])

    def result():
        return jnp.concatenate([acc_sc[0] / l_sc[0], acc_sc[1] / l_sc[1]], axis=0)

    l_sc[...] = jnp.zeros_like(l_sc)
    acc_sc[...] = jnp.zeros_like(acc_sc)
    q0 = pl.multiple_of(qi * tk, tk)
    for i in range(2):
        k_t = jnp.transpose(ka_ref[i, pl.ds(q0, tk), :].astype(F32))
        self_logit = jnp.sum(q_t[i * FOX_HD:(i + 1) * FOX_HD, :] * k_t[:FOX_HD, :], axis=0, keepdims=True)
        qt_sc[i] = q_aug(i, fq_ref[i:i + 1, :] - self_logit)

    def fixed_shift_blocks(kjs, causal=False):
        ps, sums = [], [0.0, 0.0]
        for kj in kjs:
            c0 = pl.multiple_of(kj * tk, tk)
            for i in range(2):
                x = _dot(ka_ref[i, pl.ds(c0, tk), :], qt_sc[i])
                if causal:
                    key = lax.broadcasted_iota(jnp.int32, (tk, tq), 0)
                    qry = lax.broadcasted_iota(jnp.int32, (tk, tq), 1)
                    x = jnp.where(key <= qry, x, NEG_BIG)
                p = jnp.exp2(x)
                sums[i] = sums[i] + jnp.sum(p, axis=0, keepdims=True)
                ps.append(p.astype(BF16))
        for i in range(2):
            l_sc[i] = l_sc[i] + sums[i]
            pv = 0.0
            for n, kj in enumerate(kjs):
                c0 = pl.multiple_of(kj * tk, tk)
                pv = pv + _dot(vt_ref[i * FOX_HD:(i + 1) * FOX_HD, pl.ds(c0, tk)], ps[2 * n + i])
            acc_sc[i] = acc_sc[i] + pv

    def quad_body(j, carry):
        fixed_shift_blocks(tuple(4 * j + n for n in range(4)))
        return carry

    lax.fori_loop(0, qi // 4, quad_body, 0)
    done = (qi // 4) * 4

    @pl.when(qi % 4 >= 2)
    def _():
        fixed_shift_blocks((done, done + 1))

    @pl.when(qi % 2 == 1)
    def _():
        fixed_shift_blocks((qi - 1,))

    fixed_shift_blocks((qi,), causal=True)
    o_t = result()
    bad = (jnp.sum(jnp.where(jnp.isfinite(o_t), 0.0, 1.0))
           + jnp.sum(jnp.where(jnp.isfinite(l_sc[...]), 0.0, 1.0)))

    @pl.when(bad == 0.0)
    def _():
        o_ref[...] = jnp.transpose(o_t)

    @pl.when(bad != 0.0)
    def _():
        reset()

        def body(kj, carry):
            online_block(kj, False)
            return carry

        lax.fori_loop(0, qi, body, 0)
        online_block(qi, True)
        o_ref[...] = jnp.transpose(result())


def _fox_attn(q, ka, vt, fq, q_s, k_new, v_new, lf_new, cache_k, cache_v, cache_logf, page_table):
    bsz, length, d_b = q.shape
    pairs = fq.shape[1]
    tq = min(ATTN_TK, length)
    nq = length // tq
    n, n_pages = page_table.shape
    _, ps, nh, hd = cache_k.shape
    steps_per_sample, rem = divmod(bsz * pairs * nq, n)
    assert rem == 0 and steps_per_sample >= 2, "need at least two prompt steps per sample (score step, output step)"

    def score_sample(b, p, i):
        return ((b * pairs + p) * nq + i) // steps_per_sample

    def out_sample(b, p, i):
        return jnp.maximum((b * pairs + p) * nq + i - 1, 0) // steps_per_sample

    dec_operands, dec_specs, dec_out_spec = _decode_operands(
        q_s, k_new, v_new, lf_new, cache_k, cache_v, cache_logf, score_sample, out_sample)
    resident = pl.Buffered(1)
    grid_spec = pltpu.PrefetchScalarGridSpec(
        num_scalar_prefetch=1,
        grid=(bsz, pairs, nq),
        in_specs=[
            pl.BlockSpec((None, tq, LANES), lambda b, p, i, pt: (b, i, p)),
            pl.BlockSpec((None, 2, length, LANES), lambda b, p, i, pt: (b, p, 0, 0), pipeline_mode=resident),
            pl.BlockSpec((None, LANES, length), lambda b, p, i, pt: (b, p, 0), pipeline_mode=resident),
            pl.BlockSpec((None, None, 2, tq), lambda b, p, i, pt: (b, p, 0, i)),
        ] + dec_specs,
        out_specs=[pl.BlockSpec((None, tq, LANES), lambda b, p, i, pt: (b, i, p)), dec_out_spec],
        scratch_shapes=[pltpu.VMEM((2, LANES, tq), BF16),
                        pltpu.VMEM((2, 1, tq), F32), pltpu.VMEM((2, 1, tq), F32),
                        pltpu.VMEM((2, FOX_HD, tq), F32),
                        pltpu.VMEM((nh, n_pages * ps), F32), pltpu.VMEM((nh, LANES), F32),
                        pltpu.VMEM((n_pages, nh, hd, ps), F32), pltpu.VMEM((n_pages, nh, hd, ps), F32),
                        pltpu.VMEM((n_pages, nh, ps), F32), pltpu.SemaphoreType.DMA((3,))],
    )
    o_p, o_t = pl.pallas_call(
        functools.partial(_fox_attn_kernel, n_samples=n, steps_per_sample=steps_per_sample),
        grid_spec=grid_spec,
        out_shape=[jax.ShapeDtypeStruct((bsz, length, d_b), F32), jax.ShapeDtypeStruct((n, hd, nh), F32)],
        compiler_params=_params(("arbitrary", "arbitrary", "arbitrary")),
        name="fox_attn",
    )(page_table.reshape(-1), q, ka, vt, fq, *dec_operands)
    return o_p, o_t.transpose(0, 2, 1).reshape(n, nh * hd)


def _decode_scores(refs, s_sc, st_sc, n_pages):
    k_pages = refs[:n_pages]
    f_pages = refs[n_pages:2 * n_pages]
    qt_ref, q_ref, kn_ref, lfn_ref = refs[2 * n_pages:]
    nh, hd, ps = k_pages[0].shape

    later = lax.broadcasted_iota(jnp.int32, (ps, ps), 0) > lax.broadcasted_iota(jnp.int32, (ps, ps), 1)
    later = jnp.where(later, 1.0, 0.0).astype(BF16)
    carry = lfn_ref[...]
    for j in reversed(range(n_pages)):
        page = f_pages[j][...]
        within = sum(_dot(piece, later) for piece in _split3(page))
        s_sc[:, j * ps:(j + 1) * ps] = (within + carry) * LOG2E
        carry = carry + jnp.sum(page, axis=-1, keepdims=True)

    for h in range(nh):
        q_col = jnp.broadcast_to(qt_ref[:, h:h + 1].astype(F32), (hd, ps))
        for j in range(n_pages):
            cols = slice(j * ps, (j + 1) * ps)
            s_sc[h:h + 1, cols] = s_sc[h:h + 1, cols] + jnp.sum(k_pages[j][h] * q_col, axis=0, keepdims=True)
    s_self = jnp.sum(q_ref[...].astype(F32) * kn_ref[...], axis=-1, keepdims=True)
    s = s_sc[...]
    m = jnp.maximum(s_self, jnp.max(s, axis=-1, keepdims=True))
    p = jnp.exp2(s - m)
    p_self = jnp.exp2(s_self - m)
    denom = p_self + jnp.sum(p, axis=-1, keepdims=True)
    s_sc[...] = p
    st_sc[:, 0:1] = p_self
    st_sc[:, 1:2] = denom


def _decode_output(refs, o_ref, s_sc, st_sc, n_pages):
    v_pages = refs[:n_pages]
    vnt_ref = refs[n_pages]
    nh, hd, ps = v_pages[0].shape
    p_self = st_sc[:, 0:1]
    denom = st_sc[:, 1:2]
    head_lane = lax.broadcasted_iota(jnp.int32, (1, nh), 1)
    o_t = jnp.zeros((hd, nh), F32)
    for h in range(nh):
        acc = jnp.zeros((hd, ps), F32)
        for j in range(n_pages):
            acc = acc + v_pages[j][h] * s_sc[h:h + 1, j * ps:(j + 1) * ps]
        o_t = jnp.where(head_lane == h, jnp.sum(acc, axis=-1, keepdims=True), o_t)

    def as_row(col):
        eye = lax.broadcasted_iota(jnp.int32, (nh, nh), 0) == lax.broadcasted_iota(jnp.int32, (nh, nh), 1)
        return jnp.sum(jnp.where(eye, col, 0.0), axis=0, keepdims=True)

    o_ref[...] = (o_t + vnt_ref[...] * as_row(p_self)) / as_row(denom)


N_DEC_OPERANDS = 8


def _decode_operands(q, k_new, v_new, lf_new, cache_k, cache_v, cache_logf, score_sample, out_sample):
    _, _, nh, hd = cache_k.shape
    views = [cache_k.transpose(0, 2, 3, 1), cache_v.transpose(0, 2, 3, 1), cache_logf.transpose(0, 2, 1)]

    def per_sample(shape, which):
        return pl.BlockSpec((None,) + shape, lambda *a: (which(*a[:-1]), 0, 0))

    specs = ([pl.BlockSpec(memory_space=pl.ANY)] * len(views)
             + [per_sample((hd, nh), score_sample), per_sample((nh, hd), score_sample),
                per_sample((nh, hd), score_sample), per_sample((nh, 1), score_sample),
                per_sample((hd, nh), out_sample)])
    operands = views + [q.transpose(0, 2, 1), q, k_new, lf_new[:, :, None], v_new.transpose(0, 2, 1)]
    return operands, specs, per_sample((hd, nh), out_sample)


def kernel(x_prompt, x_sample, c_prompt, c_sample, state_gla, cache_k, cache_v, cache_logf, page_table,
           ada_w, ada_b, gla_w_in, gla_w_g2, gla_b_g, gla_onorm, gla_w_out, kv_ada_w, kv_ada_b, w_kv, b_f,
           k_norm, fox_w_in, q_norm, fox_w_out):
    bp, seq, d = x_prompt.shape
    ns = x_sample.shape[0]
    n_a = gla_w_in.shape[0]
    depth = ada_w.shape[0]
    nh = b_f.shape[0]
    d_b = nh * FOX_HD
    dk, dv = state_gla.shape[-2:]

    pad = (-(ns + bp)) % 8
    c_all = jnp.concatenate([c_sample, c_prompt, jnp.zeros((pad, d), F32)], axis=0)
    mod = _ada(c_all, ada_w, ada_b)
    kv_mod = _ada(c_all, kv_ada_w[None], kv_ada_b[None])[0]

    def split_mod(m, n_parts):
        parts = jnp.split(m, n_parts, axis=-1)
        return ([p[ns:ns + bp, None, :] for p in parts],
                [p[None, :ns, :] for p in parts])

    xp = x_prompt
    xs = x_sample.reshape(1, ns, d)
    sp_states, ss_states = [], None
    for layer in range(n_a):
        mod_p, mod_s = split_mod(mod[layer], 3)
        q, k, v, g, la = _gla_in(xp, mod_p[0], mod_p[1], gla_w_in[layer], gla_w_g2[layer], gla_b_g[layer])
        o, s_fin = _gla_scan(q, k, v, la)
        sp_states.append(s_fin)
        xp = _out_proj(o, g, xp, mod_p[2], gla_w_out[layer], gla_onorm[layer])
        q, k, v, g, la = _gla_in(xs, mod_s[0], mod_s[1], gla_w_in[layer], gla_w_g2[layer], gla_b_g[layer])
        o, ss_states = _gla_step(q[0], k[0], la[0], v[0], state_gla, layer, ss_states)
        xs = _out_proj(o.reshape(1, ns, -1), g, xs, mod_s[2], gla_w_out[layer], gla_onorm[layer])

    kvm_p, kvm_s = split_mod(kv_mod, 2)
    kt_p, vt_p, lft_p, lf_p, kb_p, vtb_p = _kv_proj(xp, kvm_p[0], kvm_p[1], w_kv, b_f, k_norm)
    kt_s, vt_s, lft_s, _, _, _ = _kv_proj(xs, kvm_s[0], kvm_s[1], w_kv, b_f, k_norm)
    f2_p = _cumsum_rows(lf_p)
    ka_p = _key_aug(kb_p, f2_p)
    fq_p = f2_p.reshape(bp, seq, nh // 2, 2).transpose(0, 2, 3, 1)
    k_new = kt_s[0].transpose(2, 0, 1)
    v_new = vt_s[0].transpose(2, 0, 1)
    lf_new = lft_s[0].T

    for j in range(depth - n_a):
        mod_p, mod_s = split_mod(mod[n_a + j], 3)
        q_p, g_p = _fox_in(xp, mod_p[0], mod_p[1], fox_w_in[j], q_norm[j])
        q_s, g_s = _fox_in(xs, mod_s[0], mod_s[1], fox_w_in[j], q_norm[j])
        o_p, o_s = _fox_attn(q_p, ka_p, vtb_p, fq_p, q_s.reshape(ns, nh, FOX_HD), k_new, v_new, lf_new,
                             cache_k, cache_v, cache_logf, page_table)
        xp = _out_proj(o_p, g_p, xp, mod_p[2], fox_w_out[j])
        xs = _out_proj(o_s.reshape(1, ns, d_b), g_s, xs, mod_s[2], fox_w_out[j])

    return (xp, xs.reshape(ns, 1, d),
            jnp.stack(sp_states), ss_states,
            kt_p.transpose(0, 3, 1, 2), vt_p.transpose(0, 3, 1, 2), lft_p.transpose(0, 2, 1),
            k_new[:, None], v_new[:, None], lf_new[:, None])
```

```python
import functools

import jax
import jax.numpy as jnp
from jax import lax
from jax.experimental import pallas as pl
from jax.experimental.pallas import tpu as pltpu

F32 = jnp.float32
BF16 = jnp.bfloat16

EPS = 1e-6
GATE_TAU = 16.0
GLA_HEADS = 4
GLA_CHUNK = 64
GLA_SUB = 16
FOX_HD = 64
LANES = 128
NEG_BIG = -1e30
LOG2E = 1.4426950408889634
VMEM_LIMIT = 56 * 1024 * 1024

ROW_TILE = 512
ATTN_TK = 512
SCAN_T = 512
GLA_STEP_G = 4


def _params(sem):
    return pltpu.CompilerParams(dimension_semantics=sem, vmem_limit_bytes=VMEM_LIMIT)


def _silu(x):
    return x / (1.0 + jnp.exp(-x))


def _log_sigmoid(x):
    return jnp.minimum(x, 0.0) - jnp.log1p(jnp.exp(-jnp.abs(x)))


def _rms(x):
    return x * lax.rsqrt(jnp.mean(x * x, axis=-1, keepdims=True) + EPS)


def _dot(a, b):
    return jnp.dot(a, b, preferred_element_type=F32)


def _dot_nt(a, b):
    return lax.dot_general(a, b, (((1,), (1,)), ((), ())), preferred_element_type=F32)


def _dot_tn(a, b):
    return lax.dot_general(a, b, (((0,), (0,)), ((), ())), preferred_element_type=F32)


def _split3(x):
    p0 = x.astype(BF16)
    r1 = x - p0.astype(F32)
    p1 = r1.astype(BF16)
    p2 = (r1 - p1.astype(F32)).astype(BF16)
    return p0, p1, p2


def _exact_left_mul(mat01, x):
    p0, p1, p2 = _split3(x)
    return _dot(mat01, p0) + _dot(mat01, p1) + _dot(mat01, p2)


def _modulated_norm(x_ref, shift_ref, scale_ref):
    h = _rms(x_ref[...]) * (1.0 + scale_ref[...]) + shift_ref[...]
    return h.astype(BF16)


def _head_rms64(p):
    lane = lax.broadcasted_iota(jnp.int32, (1, LANES), 1)
    low = lane < FOX_HD
    outs = []
    for c in range(p.shape[-1] // LANES):
        blk = p[:, c * LANES:(c + 1) * LANES]
        sq = blk * blk
        s_lo = jnp.sum(jnp.where(low, sq, 0.0), axis=-1, keepdims=True)
        s_hi = jnp.sum(jnp.where(low, 0.0, sq), axis=-1, keepdims=True)
        r_lo = lax.rsqrt(s_lo * (1.0 / FOX_HD) + EPS)
        r_hi = lax.rsqrt(s_hi * (1.0 / FOX_HD) + EPS)
        outs.append(blk * jnp.where(low, r_lo, r_hi))
    return jnp.concatenate(outs, axis=-1)


def _ada_kernel(c_ref, w_ref, b_ref, o_ref):
    a = _silu(c_ref[...]).astype(BF16)
    o_ref[...] = _dot(a, w_ref[...].astype(BF16)) + b_ref[...]


def _ada(c_all, w, b, tn=1024):
    nl, d, n = w.shape
    rows = c_all.shape[0]
    return pl.pallas_call(
        _ada_kernel,
        grid=(nl, n // tn),
        in_specs=[
            pl.BlockSpec((rows, d), lambda l, j: (0, 0)),
            pl.BlockSpec((None, d, tn), lambda l, j: (l, 0, j)),
            pl.BlockSpec((None, 1, tn), lambda l, j: (l, 0, j)),
        ],
        out_specs=pl.BlockSpec((None, rows, tn), lambda l, j: (l, 0, j)),
        out_shape=jax.ShapeDtypeStruct((nl, rows, n), F32),
        compiler_params=_params(("parallel", "parallel")),
        name="ada_mod",
    )(c_all, w, b.reshape(nl, 1, n))


def _row_spec(tm, width):
    return pl.BlockSpec((None, tm, width), lambda b, i: (b, i, 0))


def _mod_spec(mod, tm):
    if mod.shape[1] == 1:
        return pl.BlockSpec((None, 1, mod.shape[2]), lambda b, i: (b, 0, 0))
    return pl.BlockSpec((None, tm, mod.shape[2]), lambda b, i: (b, i, 0))


def _full_spec(a):
    nd = a.ndim
    return pl.BlockSpec(a.shape, lambda b, i: (0,) * nd)


def _row_tile(length):
    return min(ROW_TILE, length)


def _gla_in_kernel(x_ref, sh_ref, sc_ref, wq_ref, wk_ref, wv_ref, wg_ref, wz_ref, wg2_ref, bg_ref,
                   q_ref, k_ref, v_ref, g_ref, la_ref, *, q_scale):
    h = _modulated_norm(x_ref, sh_ref, sc_ref)
    q_ref[...] = _dot(h, wq_ref[...]) * q_scale
    k_ref[...] = _dot(h, wk_ref[...])
    v_ref[...] = _dot(h, wv_ref[...])
    g_ref[...] = _dot(h, wg_ref[...])
    z = _dot(h, wz_ref[...])
    t = _dot(z.astype(BF16), wg2_ref[...]) + bg_ref[...]
    la_ref[...] = _log_sigmoid(t) * (1.0 / GATE_TAU)


def _gla_in(x, shift, scale, w_in, w_g2, b_g):
    bsz, length, d = x.shape
    dk_tot = w_g2.shape[1]
    rank = w_g2.shape[0]
    dv_tot = (w_in.shape[1] - 2 * dk_tot - rank) // 2
    dk = dk_tot // GLA_HEADS
    wb = w_in.astype(BF16)
    wq = wb[:, :dk_tot]
    wk = wb[:, dk_tot:2 * dk_tot]
    wv = wb[:, 2 * dk_tot:2 * dk_tot + dv_tot]
    wg = wb[:, 2 * dk_tot + dv_tot:2 * dk_tot + 2 * dv_tot]
    wz = wb[:, 2 * dk_tot + 2 * dv_tot:]
    wg2 = w_g2.astype(BF16)
    bg = b_g.reshape(1, dk_tot)
    tm = _row_tile(length)
    weights = (wq, wk, wv, wg, wz, wg2, bg)
    widths = (dk_tot, dk_tot, dv_tot, dv_tot, dk_tot)
    return pl.pallas_call(
        functools.partial(_gla_in_kernel, q_scale=dk ** -0.5),
        grid=(bsz, length // tm),
        in_specs=[_row_spec(tm, d), _mod_spec(shift, tm), _mod_spec(scale, tm)]
        + [_full_spec(w) for w in weights],
        out_specs=[_row_spec(tm, w) for w in widths],
        out_shape=[jax.ShapeDtypeStruct((bsz, length, w), F32) for w in widths],
        compiler_params=_params(("parallel", "parallel")),
        name="gla_in_proj",
    )(x, shift, scale, *weights)


def _gla_chunk_head(q, k, v, b, s_old, rows):
    c, dk = q.shape
    nsub = c // GLA_SUB
    b_last = b[c - 1:c, :]
    o = _dot((q * jnp.exp(b)).astype(BF16), s_old.astype(BF16))
    kd = k * jnp.exp(b_last - b)
    upd = _dot_tn(kd.astype(BF16), v.astype(BF16))
    decay_col = jnp.exp(jnp.transpose(jnp.broadcast_to(b_last, (dk, dk))))
    s_new = jnp.concatenate([decay_col] * (s_old.shape[1] // dk), axis=1) * s_old + upd
    qcat, kcat = [], []
    for j in range(nsub - 1):
        r_j = b[(j + 1) * GLA_SUB - 1:(j + 1) * GLA_SUB, :]
        in_j = (rows >= j * GLA_SUB) & (rows < (j + 1) * GLA_SUB)
        after_j = rows >= (j + 1) * GLA_SUB
        kcat.append(jnp.where(in_j, k * jnp.exp(jnp.where(in_j, r_j - b, 0.0)), 0.0))
        qcat.append(jnp.where(after_j, q * jnp.exp(jnp.where(after_j, b - r_j, 0.0)), 0.0))
    att = _dot_nt(jnp.concatenate(qcat, axis=1).astype(BF16), jnp.concatenate(kcat, axis=1).astype(BF16))
    sub_row = lax.broadcasted_iota(jnp.int32, (GLA_SUB, 1), 0)
    col_id = lax.broadcasted_iota(jnp.int32, (1, c), 1)
    diag_blocks = []
    for j in range(nsub):
        lo = j * GLA_SUB
        qj = q[lo:lo + GLA_SUB, :]
        bj = b[lo:lo + GLA_SUB, :]
        blk = jnp.zeros((GLA_SUB, c), F32)
        for t in range(GLA_SUB):
            src = lo + t
            valid = sub_row >= t
            e = jnp.exp(jnp.where(valid, bj - b[src:src + 1, :], 0.0))
            col = jnp.sum(qj * k[src:src + 1, :] * e, axis=-1, keepdims=True)
            blk = jnp.where((col_id == src) & valid, col, blk)
        diag_blocks.append(blk)
    att = att + jnp.concatenate(diag_blocks, axis=0)
    o = o + _dot(att.astype(BF16), v.astype(BF16))
    return o, s_new


def _gla_scan_kernel(q_ref, k_ref, v_ref, la_ref, o_ref, sfin_ref, s_sc, *, n_chunks):
    t = pl.program_id(1)
    dk = q_ref.shape[-1] // GLA_HEADS
    dv = v_ref.shape[-1] // GLA_HEADS
    c = GLA_CHUNK

    @pl.when(t == 0)
    def _():
        s_sc[...] = jnp.zeros_like(s_sc)

    tri = (lax.broadcasted_iota(jnp.int32, (c, c), 0) >= lax.broadcasted_iota(jnp.int32, (c, c), 1))
    tri = jnp.where(tri, 1.0, 0.0).astype(BF16)
    rows = lax.broadcasted_iota(jnp.int32, (c, 1), 0)

    def chunk(ci, carry):
        r0 = pl.multiple_of(ci * c, c)
        b_all = _exact_left_mul(tri, la_ref[pl.ds(r0, c), :])
        for h in range(GLA_HEADS):
            o, s_new = _gla_chunk_head(
                q_ref[pl.ds(r0, c), h * dk:(h + 1) * dk],
                k_ref[pl.ds(r0, c), h * dk:(h + 1) * dk],
                v_ref[pl.ds(r0, c), h * dv:(h + 1) * dv],
                b_all[:, h * dk:(h + 1) * dk],
                s_sc[h], rows)
            o_ref[pl.ds(r0, c), h * dv:(h + 1) * dv] = o
            s_sc[h] = s_new
        return carry

    lax.fori_loop(0, n_chunks, chunk, 0)

    @pl.when(t == pl.num_programs(1) - 1)
    def _():
        sfin_ref[...] = s_sc[...]


def _gla_scan(q, k, v, la):
    bsz, length, dk_tot = q.shape
    dv_tot = v.shape[-1]
    dk, dv = dk_tot // GLA_HEADS, dv_tot // GLA_HEADS
    tt = min(SCAN_T, length)
    return pl.pallas_call(
        functools.partial(_gla_scan_kernel, n_chunks=tt // GLA_CHUNK),
        grid=(bsz, length // tt),
        in_specs=[_row_spec(tt, dk_tot), _row_spec(tt, dk_tot), _row_spec(tt, dv_tot), _row_spec(tt, dk_tot)],
        out_specs=[_row_spec(tt, dv_tot),
                   pl.BlockSpec((None, GLA_HEADS, dk, dv), lambda b, i: (b, 0, 0, 0))],
        out_shape=[jax.ShapeDtypeStruct((bsz, length, dv_tot), F32),
                   jax.ShapeDtypeStruct((bsz, GLA_HEADS, dk, dv), F32)],
        scratch_shapes=[pltpu.VMEM((GLA_HEADS, dk, dv), F32)],
        compiler_params=_params(("parallel", "arbitrary")),
        name="gla_scan",
    )(q, k, v, la)


def _gla_step_kernel(qt_ref, kt_ref, lat_ref, v_ref, s_ref, *rest, n_prev):
    if n_prev:
        prev_ref, o_ref, so_ref = rest
        so_ref[0:n_prev] = prev_ref[...]
    else:
        o_ref, so_ref = rest
    dk, dv = s_ref.shape[-2:]
    for g in range(s_ref.shape[0]):
        for h in range(GLA_HEADS):
            q_col = qt_ref[h * dk:(h + 1) * dk, g:g + 1]
            k_col = kt_ref[h * dk:(h + 1) * dk, g:g + 1]
            a_col = jnp.exp(lat_ref[h * dk:(h + 1) * dk, g:g + 1])
            v_row = v_ref[g:g + 1, h * dv:(h + 1) * dv]
            s_new = a_col * s_ref[g, h] + k_col * v_row
            so_ref[n_prev, g, h] = s_new
            o_ref[g:g + 1, h * dv:(h + 1) * dv] = jnp.sum(q_col * s_new, axis=0, keepdims=True)


def _gla_step(q, k, la, v, states, layer, prev=None):
    n, dk_tot = q.shape
    dv_tot = v.shape[-1]
    dk, dv = states.shape[-2:]
    n_prev = 0 if prev is None else prev.shape[0]
    g = GLA_STEP_G
    steps = n // g

    def cols(a):
        return a.reshape(steps, g, a.shape[-1]).transpose(0, 2, 1)

    col_spec = pl.BlockSpec((None, dk_tot, g), lambda i: (i, 0, 0))
    prev_in = [] if prev is None else [prev]
    prev_spec = [] if prev is None else [pl.BlockSpec((n_prev, g, GLA_HEADS, dk, dv), lambda i: (0, i, 0, 0, 0))]
    o, stacked = pl.pallas_call(
        functools.partial(_gla_step_kernel, n_prev=n_prev),
        grid=(steps,),
        in_specs=[col_spec, col_spec, col_spec,
                  pl.BlockSpec((None, g, dv_tot), lambda i: (i, 0, 0)),
                  pl.BlockSpec((None, g, GLA_HEADS, dk, dv), lambda i: (layer, i, 0, 0, 0))] + prev_spec,
        out_specs=[pl.BlockSpec((None, g, dv_tot), lambda i: (i, 0, 0)),
                   pl.BlockSpec((n_prev + 1, g, GLA_HEADS, dk, dv), lambda i: (0, i, 0, 0, 0))],
        out_shape=[jax.ShapeDtypeStruct((steps, g, dv_tot), F32),
                   jax.ShapeDtypeStruct((n_prev + 1,) + states.shape[1:], F32)],
        compiler_params=_params(("parallel",)),
        name="gla_step",
    )(cols(q), cols(k), cols(la), v.reshape(steps, g, dv_tot), states, *prev_in)
    return o.reshape(n, dv_tot), stacked


def _out_kernel(o_ref, g_ref, x_ref, gate_ref, w_ref, *rest, head_dim):
    if head_dim:
        onorm_ref, y_ref = rest
        o = o_ref[...]
        o = jnp.concatenate(
            [_rms(o[:, h * head_dim:(h + 1) * head_dim]) for h in range(o.shape[-1] // head_dim)], axis=-1)
        o = o * onorm_ref[...]
    else:
        (y_ref,) = rest
        o = o_ref[...]
    a = (o * _silu(g_ref[...])).astype(BF16)
    y_ref[...] = x_ref[...] + gate_ref[...] * _dot(a, w_ref[...])


def _out_proj(o, g, x, gate, w_out, onorm=None):
    bsz, length, d = x.shape
    width = o.shape[-1]
    tm = _row_tile(length)
    wb = w_out.astype(BF16)
    extra, head_dim = [], 0
    if onorm is not None:
        head_dim = onorm.shape[0]
        extra = [jnp.tile(onorm, width // head_dim).reshape(1, width)]
    return pl.pallas_call(
        functools.partial(_out_kernel, head_dim=head_dim),
        grid=(bsz, length // tm),
        in_specs=[_row_spec(tm, width), _row_spec(tm, width), _row_spec(tm, d), _mod_spec(gate, tm),
                  _full_spec(wb)] + [_full_spec(e) for e in extra],
        out_specs=_row_spec(tm, d),
        out_shape=jax.ShapeDtypeStruct((bsz, length, d), F32),
        compiler_params=_params(("parallel", "parallel")),
        name="out_proj",
    )(o, g, x, gate, wb, *extra)


def _kv_kernel(x_ref, sh_ref, sc_ref, wk_ref, wv_ref, wf_ref, wft_ref, bf_ref, bft_ref, kn_ref,
               kt_ref, vt_ref, lft_ref, lf_ref, kb_ref, vtb_ref):
    h = _modulated_norm(x_ref, sh_ref, sc_ref)
    k = _head_rms64(_dot(h, wk_ref[...])) * kn_ref[...]
    v = _dot(h, wv_ref[...])
    k_t = jnp.transpose(k)
    v_t = jnp.transpose(v)
    for hd in range(kt_ref.shape[0]):
        kt_ref[hd] = k_t[hd * FOX_HD:(hd + 1) * FOX_HD, :]
        vt_ref[hd] = v_t[hd * FOX_HD:(hd + 1) * FOX_HD, :]
    kb_ref[...] = k.astype(BF16)
    vtb_ref[...] = v_t.astype(BF16)
    lf_ref[...] = _log_sigmoid(_dot(h, wf_ref[...]) + bf_ref[...])
    lft_ref[...] = _log_sigmoid(_dot_nt(wft_ref[...], h) + bft_ref[...])


def _kv_proj(x, shift, scale, w_kv, b_f, k_norm):
    bsz, length, d = x.shape
    nh = b_f.shape[0]
    d_b = (w_kv.shape[1] - nh) // 2
    hd = d_b // nh
    wb = w_kv.astype(BF16)
    weights = (wb[:, :d_b], wb[:, d_b:2 * d_b], wb[:, 2 * d_b:], wb[:, 2 * d_b:].T,
               b_f.reshape(1, nh), b_f.reshape(nh, 1), jnp.tile(k_norm, nh).reshape(1, d_b))
    tm = _row_tile(length)
    tok_minor = pl.BlockSpec((None, nh, hd, tm), lambda b, i: (b, 0, 0, i))
    return pl.pallas_call(
        _kv_kernel,
        grid=(bsz, length // tm),
        in_specs=[_row_spec(tm, d), _mod_spec(shift, tm), _mod_spec(scale, tm)]
        + [_full_spec(w) for w in weights],
        out_specs=[tok_minor, tok_minor,
                   pl.BlockSpec((None, nh, tm), lambda b, i: (b, 0, i)),
                   _row_spec(tm, nh), _row_spec(tm, d_b),
                   pl.BlockSpec((None, d_b, tm), lambda b, i: (b, 0, i))],
        out_shape=[jax.ShapeDtypeStruct((bsz, nh, hd, length), F32),
                   jax.ShapeDtypeStruct((bsz, nh, hd, length), F32),
                   jax.ShapeDtypeStruct((bsz, nh, length), F32),
                   jax.ShapeDtypeStruct((bsz, length, nh), F32),
                   jax.ShapeDtypeStruct((bsz, length, d_b), BF16),
                   jax.ShapeDtypeStruct((bsz, d_b, length), BF16)],
        compiler_params=_params(("parallel", "parallel")),
        name="kv_proj",
    )(x, shift, scale, *weights)


def _fox_in_kernel(x_ref, sh_ref, sc_ref, wq_ref, wg_ref, qn_ref, q_ref, g_ref):
    h = _modulated_norm(x_ref, sh_ref, sc_ref)
    q_ref[...] = (_head_rms64(_dot(h, wq_ref[...])) * qn_ref[...]).astype(BF16)
    g_ref[...] = _dot(h, wg_ref[...])


def _fox_in(x, shift, scale, w_in, q_norm):
    bsz, length, d = x.shape
    d_b = w_in.shape[1] // 2
    wb = w_in.astype(BF16)
    qn = (jnp.tile(q_norm, d_b // FOX_HD) * (FOX_HD ** -0.5 * LOG2E)).reshape(1, d_b)
    weights = (wb[:, :d_b], wb[:, d_b:], qn)
    tm = _row_tile(length)
    return pl.pallas_call(
        _fox_in_kernel,
        grid=(bsz, length // tm),
        in_specs=[_row_spec(tm, d), _mod_spec(shift, tm), _mod_spec(scale, tm)]
        + [_full_spec(w) for w in weights],
        out_specs=[_row_spec(tm, d_b), _row_spec(tm, d_b)],
        out_shape=[jax.ShapeDtypeStruct((bsz, length, d_b), BF16),
                   jax.ShapeDtypeStruct((bsz, length, d_b), F32)],
        compiler_params=_params(("parallel", "parallel")),
        name="fox_in_proj",
    )(x, shift, scale, *weights)


def _cumsum_kernel(lf_ref, f_ref, carry_sc):
    r = lf_ref.shape[0]

    @pl.when(pl.program_id(1) == 0)
    def _():
        carry_sc[...] = jnp.zeros_like(carry_sc)

    tri = lax.broadcasted_iota(jnp.int32, (r, r), 0) >= lax.broadcasted_iota(jnp.int32, (r, r), 1)
    tri = jnp.where(tri, 1.0, 0.0).astype(BF16)
    f = _exact_left_mul(tri, lf_ref[...]) + carry_sc[...]
    f_ref[...] = f * LOG2E
    carry_sc[...] = f[r - 1:r, :]


def _cumsum_rows(lf):
    bsz, length, nh = lf.shape
    r = min(ROW_TILE, length)
    return pl.pallas_call(
        _cumsum_kernel,
        grid=(bsz, length // r),
        in_specs=[_row_spec(r, nh)],
        out_specs=_row_spec(r, nh),
        out_shape=jax.ShapeDtypeStruct(lf.shape, F32),
        scratch_shapes=[pltpu.VMEM((1, nh), F32)],
        compiler_params=_params(("parallel", "arbitrary")),
        name="forget_cumsum",
    )(lf)


N_PIECES = 3


def _key_aug_kernel(kb_ref, f_ref, place_k_ref, place_f_ref, ones_ref, ka_ref):
    bias = ones_ref[...]
    for j, piece in enumerate(_split3(f_ref[...])):
        bias = bias - _dot(piece, place_f_ref[j])
    for p in range(ka_ref.shape[0] // 2):
        pair = _dot(kb_ref[:, p * LANES:(p + 1) * LANES], place_k_ref[...]) + bias[:, 2 * p * LANES:2 * (p + 1) * LANES]
        ka_ref[2 * p] = pair[:, :LANES].astype(BF16)
        ka_ref[2 * p + 1] = pair[:, LANES:].astype(BF16)


def _key_aug(kb, f2):
    bsz, length, d_b = kb.shape
    nh = f2.shape[-1]
    tm = _row_tile(length)
    col = jnp.arange(nh * LANES)
    row = jnp.arange(LANES)
    place_k = ((col[None, :2 * LANES] // LANES == row[:, None] // FOX_HD)
               & (col[None, :2 * LANES] % LANES == row[:, None] % FOX_HD)).astype(BF16)
    head = jnp.arange(nh)
    place_f = jnp.stack([(col[None, :] == head[:, None] * LANES + FOX_HD + j) for j in range(N_PIECES)]
                        ).astype(BF16)
    ones = ((col % LANES >= FOX_HD + N_PIECES) & (col % LANES < FOX_HD + 2 * N_PIECES)).astype(F32)[None, :]
    return pl.pallas_call(
        _key_aug_kernel,
        grid=(bsz, length // tm),
        in_specs=[_row_spec(tm, d_b), _row_spec(tm, nh), _full_spec(place_k), _full_spec(place_f),
                  _full_spec(ones)],
        out_specs=pl.BlockSpec((None, nh, tm, LANES), lambda b, i: (b, 0, i, 0)),
        out_shape=jax.ShapeDtypeStruct((bsz, nh, length, LANES), BF16),
        compiler_params=_params(("parallel", "parallel")),
        name="fox_key_aug",
    )(kb, f2, place_k, place_f, ones)


def _page_copies(pt_ref, sample, src_hbm, buf, sem):
    n_pages = buf.shape[0]
    return [pltpu.make_async_copy(src_hbm.at[pt_ref[sample * n_pages + j]], buf.at[j], sem)
            for j in range(n_pages)]


def _fox_attn_kernel(first_ref, pt_ref, q_ref, ka_ref, vt_ref, fq_ref, *rest, n_samples, steps_per_sample):
    kc_hbm, vc_hbm, fc_hbm, qt_ref, qs_ref, kn_ref, lfn_ref, vnt_ref = rest[:N_DEC_OPERANDS]
    o_ref, od_ref, qt_sc, m_sc, l_sc, acc_sc, s_sc, st_sc, kbuf, vbuf, fbuf, sems = rest[N_DEC_OPERANDS:]
    n_pages = kbuf.shape[0]
    step = (pl.program_id(0) * pl.num_programs(1) + pl.program_id(1)) * pl.num_programs(2) + pl.program_id(2)
    sample = step // steps_per_sample
    phase = step % steps_per_sample

    def score_copies(s):
        return _page_copies(pt_ref, s, kc_hbm, kbuf, sems.at[0]) + _page_copies(pt_ref, s, fc_hbm, fbuf, sems.at[1])

    def value_copies(s):
        return _page_copies(pt_ref, s, vc_hbm, vbuf, sems.at[2])

    @pl.when(step == 0)
    def _():
        for cp in score_copies(0):
            cp.start()

    @pl.when(phase == 0)
    def _():
        for cp in value_copies(sample):
            cp.start()
        for cp in score_copies(sample):
            cp.wait()
        _decode_scores([kbuf.at[j] for j in range(n_pages)] + [fbuf.at[j] for j in range(n_pages)]
                       + [qt_ref, qs_ref, kn_ref, lfn_ref], s_sc, st_sc, n_pages)

    @pl.when(phase == 1)
    def _():
        @pl.when(sample + 1 < n_samples)
        def _():
            for cp in score_copies(sample + 1):
                cp.start()
        for cp in value_copies(sample):
            cp.wait()
        _decode_output([vbuf.at[j] for j in range(n_pages)] + [vnt_ref], od_ref, s_sc, st_sc, n_pages)

    qi = pl.program_id(2)
    tq = q_ref.shape[0]
    tk = tq
    q_t = jnp.transpose(q_ref[...].astype(F32))
    brow = lax.broadcasted_iota(jnp.int32, (FOX_HD, 1), 0)
    ones_rows = jnp.broadcast_to(jnp.where(brow < N_PIECES, 1.0, 0.0), (FOX_HD, tq))

    def q_aug(i, c):
        extra = ones_rows
        if c is not None:
            for j, piece in enumerate(_split3(c)):
                extra = jnp.where(brow == N_PIECES + j, piece.astype(F32), extra)
        return jnp.concatenate([q_t[i * FOX_HD:(i + 1) * FOX_HD, :], extra], axis=0).astype(BF16)

    q_plain = (q_aug(0, None), q_aug(1, None))

    def reset():
        m_sc[...] = jnp.full_like(m_sc, NEG_BIG)
        l_sc[...] = jnp.zeros_like(l_sc)
        acc_sc[...] = jnp.zeros_like(acc_sc)

    def online_block(kj, causal):
        c0 = pl.multiple_of(kj * tk, tk)
        ts = []
        for i in range(2):
            t = _dot(ka_ref[i, pl.ds(c0, tk), :], q_plain[i])
            if causal:
                key = lax.broadcasted_iota(jnp.int32, (tk, tq), 0)
                qry = lax.broadcasted_iota(jnp.int32, (tk, tq), 1)
                t = jnp.where(key <= qry, t, NEG_BIG)
            ts.append(t)
        ps, alphas = [], []
        for i in range(2):
            fq = fq_ref[i:i + 1, :]
            m_old = m_sc[i]
            m_new = jnp.maximum(m_old, jnp.max(ts[i], axis=0, keepdims=True) + fq)
            alpha = jnp.exp2(m_old - m_new)
            p = jnp.exp2(ts[i] + (fq - m_new))
            l_sc[i] = alpha * l_sc[i] + jnp.sum(p, axis=0, keepdims=True)
            m_sc[i] = m_new
            ps.append(p.astype(BF16))
            alphas.append(alpha)
        for i in range(2):
            acc_sc[i] = alphas[i] * acc_sc[i] + _dot(vt_ref[i * FOX_HD:(i + 1) * FOX_HD, pl.ds(c0, tk)], ps[i])

    def result():
        return jnp.concatenate([acc_sc[0] / l_sc[0], acc_sc[1] / l_sc[1]], axis=0)

    l_sc[...] = jnp.zeros_like(l_sc)
    acc_sc[...] = jnp.zeros_like(acc_sc)
    q0 = pl.multiple_of(qi * tk, tk)
    for i in range(2):
        k_t = jnp.transpose(ka_ref[i, pl.ds(q0, tk), :].astype(F32))
        self_logit = jnp.sum(q_t[i * FOX_HD:(i + 1) * FOX_HD, :] * k_t[:FOX_HD, :], axis=0, keepdims=True)
        qt_sc[i] = q_aug(i, fq_ref[i:i + 1, :] - self_logit)

    def fixed_shift_blocks(i, kjs, causal=False):
        ps, total = [], 0.0
        for kj in kjs:
            c0 = pl.multiple_of(kj * tk, tk)
            x = _dot(ka_ref[i, pl.ds(c0, tk), :], qt_sc[i])
            if causal:
                key = lax.broadcasted_iota(jnp.int32, (tk, tq), 0)
                qry = lax.broadcasted_iota(jnp.int32, (tk, tq), 1)
                x = jnp.where(key <= qry, x, NEG_BIG)
            p = jnp.exp2(x)
            total = total + jnp.sum(p, axis=0, keepdims=True)
            ps.append(p.astype(BF16))
        l_sc[i] = l_sc[i] + total
        pv = 0.0
        for n, kj in enumerate(kjs):
            c0 = pl.multiple_of(kj * tk, tk)
            pv = pv + _dot(vt_ref[i * FOX_HD:(i + 1) * FOX_HD, pl.ds(c0, tk)], ps[n])
        acc_sc[i] = acc_sc[i] + pv

    slot = (pl.program_id(0) * 2 * pl.num_programs(1) + 2 * pl.program_id(1)) * pl.num_programs(2) + qi
    for i in range(2):
        first = first_ref[slot + i * pl.num_programs(2)]
        count = qi - first

        def quad_body(j, carry, i=i, first=first):
            fixed_shift_blocks(i, tuple(first + 4 * j + n for n in range(4)))
            return carry

        lax.fori_loop(0, count // 4, quad_body, 0)
        done = first + (count // 4) * 4

        @pl.when(count % 4 >= 2)
        def _(i=i, done=done):
            fixed_shift_blocks(i, (done, done + 1))

        @pl.when(count % 2 == 1)
        def _(i=i):
            fixed_shift_blocks(i, (qi - 1,))

        fixed_shift_blocks(i, (qi,), causal=True)
    o_t = result()
    bad = (jnp.sum(jnp.where(jnp.isfinite(o_t), 0.0, 1.0))
           + jnp.sum(jnp.where(jnp.isfinite(l_sc[...]), 0.0, 1.0)))

    @pl.when(bad == 0.0)
    def _():
        o_ref[...] = jnp.transpose(o_t)

    @pl.when(bad != 0.0)
    def _():
        reset()

        def body(kj, carry):
            online_block(kj, False)
            return carry

        lax.fori_loop(0, qi, body, 0)
        online_block(qi, True)
        o_ref[...] = jnp.transpose(result())


def _first_visible_block(f2, tk, qk_bound):
    bsz, length, nh = f2.shape
    f_last = f2[:, tk - 1::tk, :]
    f_first = f2[:, ::tk, :]
    gap = f_last[:, None, :, :] - f_first[:, :, None, :]
    nk = length // tk
    earlier = jnp.arange(nk)[None, None, :, None] < jnp.arange(nk)[None, :, None, None]
    invisible = earlier & (gap > 2.0 * qk_bound + 152.0)
    return jnp.sum(invisible, axis=2).astype(jnp.int32).transpose(0, 2, 1)


def _fox_attn(q, ka, vt, fq, first_block, q_s, k_new, v_new, lf_new, cache_k, cache_v, cache_logf, page_table):
    bsz, length, d_b = q.shape
    pairs = fq.shape[1]
    tq = min(ATTN_TK, length)
    nq = length // tq
    n, n_pages = page_table.shape
    _, ps, nh, hd = cache_k.shape
    steps_per_sample, rem = divmod(bsz * pairs * nq, n)
    assert rem == 0 and steps_per_sample >= 2, "need at least two prompt steps per sample (score step, output step)"

    def score_sample(b, p, i):
        return ((b * pairs + p) * nq + i) // steps_per_sample

    def out_sample(b, p, i):
        return jnp.maximum((b * pairs + p) * nq + i - 1, 0) // steps_per_sample

    dec_operands, dec_specs, dec_out_spec = _decode_operands(
        q_s, k_new, v_new, lf_new, cache_k, cache_v, cache_logf, score_sample, out_sample)
    resident = pl.Buffered(1)
    grid_spec = pltpu.PrefetchScalarGridSpec(
        num_scalar_prefetch=2,
        grid=(bsz, pairs, nq),
        in_specs=[
            pl.BlockSpec((None, tq, LANES), lambda b, p, i, first, pt: (b, i, p)),
            pl.BlockSpec((None, 2, length, LANES), lambda b, p, i, first, pt: (b, p, 0, 0), pipeline_mode=resident),
            pl.BlockSpec((None, LANES, length), lambda b, p, i, first, pt: (b, p, 0), pipeline_mode=resident),
            pl.BlockSpec((None, None, 2, tq), lambda b, p, i, first, pt: (b, p, 0, i)),
        ] + dec_specs,
        out_specs=[pl.BlockSpec((None, tq, LANES), lambda b, p, i, first, pt: (b, i, p)), dec_out_spec],
        scratch_shapes=[pltpu.VMEM((2, LANES, tq), BF16),
                        pltpu.VMEM((2, 1, tq), F32), pltpu.VMEM((2, 1, tq), F32),
                        pltpu.VMEM((2, FOX_HD, tq), F32),
                        pltpu.VMEM((nh, n_pages * ps), F32), pltpu.VMEM((nh, LANES), F32),
                        pltpu.VMEM((n_pages, nh, hd, ps), F32), pltpu.VMEM((n_pages, nh, hd, ps), F32),
                        pltpu.VMEM((n_pages, nh, ps), F32), pltpu.SemaphoreType.DMA((3,))],
    )
    o_p, o_t = pl.pallas_call(
        functools.partial(_fox_attn_kernel, n_samples=n, steps_per_sample=steps_per_sample),
        grid_spec=grid_spec,
        out_shape=[jax.ShapeDtypeStruct((bsz, length, d_b), F32), jax.ShapeDtypeStruct((n, hd, nh), F32)],
        compiler_params=_params(("arbitrary", "arbitrary", "arbitrary")),
        name="fox_attn",
    )(first_block.reshape(-1), page_table.reshape(-1), q, ka, vt, fq, *dec_operands)
    return o_p, o_t.transpose(0, 2, 1).reshape(n, nh * hd)


def _decode_scores(refs, s_sc, st_sc, n_pages):
    k_pages = refs[:n_pages]
    f_pages = refs[n_pages:2 * n_pages]
    qt_ref, q_ref, kn_ref, lfn_ref = refs[2 * n_pages:]
    nh, hd, ps = k_pages[0].shape

    later = lax.broadcasted_iota(jnp.int32, (ps, ps), 0) > lax.broadcasted_iota(jnp.int32, (ps, ps), 1)
    later = jnp.where(later, 1.0, 0.0).astype(BF16)
    carry = lfn_ref[...]
    for j in reversed(range(n_pages)):
        page = f_pages[j][...]
        within = sum(_dot(piece, later) for piece in _split3(page))
        s_sc[:, j * ps:(j + 1) * ps] = (within + carry) * LOG2E
        carry = carry + jnp.sum(page, axis=-1, keepdims=True)

    for h in range(nh):
        q_col = jnp.broadcast_to(qt_ref[:, h:h + 1].astype(F32), (hd, ps))
        for j in range(n_pages):
            cols = slice(j * ps, (j + 1) * ps)
            s_sc[h:h + 1, cols] = s_sc[h:h + 1, cols] + jnp.sum(k_pages[j][h] * q_col, axis=0, keepdims=True)
    s_self = jnp.sum(q_ref[...].astype(F32) * kn_ref[...], axis=-1, keepdims=True)
    s = s_sc[...]
    m = jnp.maximum(s_self, jnp.max(s, axis=-1, keepdims=True))
    p = jnp.exp2(s - m)
    p_self = jnp.exp2(s_self - m)
    denom = p_self + jnp.sum(p, axis=-1, keepdims=True)
    s_sc[...] = p
    st_sc[:, 0:1] = p_self
    st_sc[:, 1:2] = denom


def _decode_output(refs, o_ref, s_sc, st_sc, n_pages):
    v_pages = refs[:n_pages]
    vnt_ref = refs[n_pages]
    nh, hd, ps = v_pages[0].shape
    p_self = st_sc[:, 0:1]
    denom = st_sc[:, 1:2]
    head_lane = lax.broadcasted_iota(jnp.int32, (1, nh), 1)
    o_t = jnp.zeros((hd, nh), F32)
    for h in range(nh):
        acc = jnp.zeros((hd, ps), F32)
        for j in range(n_pages):
            acc = acc + v_pages[j][h] * s_sc[h:h + 1, j * ps:(j + 1) * ps]
        o_t = jnp.where(head_lane == h, jnp.sum(acc, axis=-1, keepdims=True), o_t)

    def as_row(col):
        eye = lax.broadcasted_iota(jnp.int32, (nh, nh), 0) == lax.broadcasted_iota(jnp.int32, (nh, nh), 1)
        return jnp.sum(jnp.where(eye, col, 0.0), axis=0, keepdims=True)

    o_ref[...] = (o_t + vnt_ref[...] * as_row(p_self)) / as_row(denom)


N_DEC_OPERANDS = 8


def _decode_operands(q, k_new, v_new, lf_new, cache_k, cache_v, cache_logf, score_sample, out_sample):
    _, _, nh, hd = cache_k.shape
    views = [cache_k.transpose(0, 2, 3, 1), cache_v.transpose(0, 2, 3, 1), cache_logf.transpose(0, 2, 1)]

    def per_sample(shape, which):
        return pl.BlockSpec((None,) + shape, lambda *a: (which(*a[:3]), 0, 0))

    specs = ([pl.BlockSpec(memory_space=pl.ANY)] * len(views)
             + [per_sample((hd, nh), score_sample), per_sample((nh, hd), score_sample),
                per_sample((nh, hd), score_sample), per_sample((nh, 1), score_sample),
                per_sample((hd, nh), out_sample)])
    operands = views + [q.transpose(0, 2, 1), q, k_new, lf_new[:, :, None], v_new.transpose(0, 2, 1)]
    return operands, specs, per_sample((hd, nh), out_sample)


def kernel(x_prompt, x_sample, c_prompt, c_sample, state_gla, cache_k, cache_v, cache_logf, page_table,
           ada_w, ada_b, gla_w_in, gla_w_g2, gla_b_g, gla_onorm, gla_w_out, kv_ada_w, kv_ada_b, w_kv, b_f,
           k_norm, fox_w_in, q_norm, fox_w_out):
    bp, seq, d = x_prompt.shape
    ns = x_sample.shape[0]
    n_a = gla_w_in.shape[0]
    depth = ada_w.shape[0]
    nh = b_f.shape[0]
    d_b = nh * FOX_HD
    dk, dv = state_gla.shape[-2:]

    pad = (-(ns + bp)) % 8
    c_all = jnp.concatenate([c_sample, c_prompt, jnp.zeros((pad, d), F32)], axis=0)
    mod = _ada(c_all, ada_w, ada_b)
    kv_mod = _ada(c_all, kv_ada_w[None], kv_ada_b[None])[0]

    def split_mod(m, n_parts):
        parts = jnp.split(m, n_parts, axis=-1)
        return ([p[ns:ns + bp, None, :] for p in parts],
                [p[None, :ns, :] for p in parts])

    xp = x_prompt
    xs = x_sample.reshape(1, ns, d)
    sp_states, ss_states = [], None
    for layer in range(n_a):
        mod_p, mod_s = split_mod(mod[layer], 3)
        q, k, v, g, la = _gla_in(xp, mod_p[0], mod_p[1], gla_w_in[layer], gla_w_g2[layer], gla_b_g[layer])
        o, s_fin = _gla_scan(q, k, v, la)
        sp_states.append(s_fin)
        xp = _out_proj(o, g, xp, mod_p[2], gla_w_out[layer], gla_onorm[layer])
        q, k, v, g, la = _gla_in(xs, mod_s[0], mod_s[1], gla_w_in[layer], gla_w_g2[layer], gla_b_g[layer])
        o, ss_states = _gla_step(q[0], k[0], la[0], v[0], state_gla, layer, ss_states)
        xs = _out_proj(o.reshape(1, ns, -1), g, xs, mod_s[2], gla_w_out[layer], gla_onorm[layer])

    kvm_p, kvm_s = split_mod(kv_mod, 2)
    kt_p, vt_p, lft_p, lf_p, kb_p, vtb_p = _kv_proj(xp, kvm_p[0], kvm_p[1], w_kv, b_f, k_norm)
    kt_s, vt_s, lft_s, _, _, _ = _kv_proj(xs, kvm_s[0], kvm_s[1], w_kv, b_f, k_norm)
    f2_p = _cumsum_rows(lf_p)
    ka_p = _key_aug(kb_p, f2_p)
    fq_p = f2_p.reshape(bp, seq, nh // 2, 2).transpose(0, 2, 3, 1)
    k_new = kt_s[0].transpose(2, 0, 1)
    v_new = vt_s[0].transpose(2, 0, 1)
    lf_new = lft_s[0].T

    for j in range(depth - n_a):
        mod_p, mod_s = split_mod(mod[n_a + j], 3)
        q_p, g_p = _fox_in(xp, mod_p[0], mod_p[1], fox_w_in[j], q_norm[j])
        q_s, g_s = _fox_in(xs, mod_s[0], mod_s[1], fox_w_in[j], q_norm[j])
        qk_bound = ((1.0 + 2.0 ** -8) ** 2 * LOG2E * FOX_HD ** 0.5
                    * jnp.max(jnp.abs(k_norm)) * jnp.max(jnp.abs(q_norm[j])))
        first_p = _first_visible_block(f2_p, min(ATTN_TK, seq), qk_bound)
        o_p, o_s = _fox_attn(q_p, ka_p, vtb_p, fq_p, first_p, q_s.reshape(ns, nh, FOX_HD), k_new, v_new, lf_new,
                             cache_k, cache_v, cache_logf, page_table)
        xp = _out_proj(o_p, g_p, xp, mod_p[2], fox_w_out[j])
        xs = _out_proj(o_s.reshape(1, ns, d_b), g_s, xs, mod_s[2], fox_w_out[j])

    return (xp, xs.reshape(ns, 1, d),
            jnp.stack(sp_states), ss_states,
            kt_p.transpose(0, 3, 1, 2), vt_p.transpose(0, 3, 1, 2), lft_p.transpose(0, 2, 1),
            k_new[:, None], v_new[:, None], lf_new[:, None])
```
